```python
import jax
import jax.numpy as jnp
from jax import lax
import numpy as np

D_MODEL = 2048
BATCH = 4
SEQ = 8192
DEPTH = 2
DEC_BATCH = 8
DEC_SEQ = 64
PAST_LEN = 1024

CHUNK = 64
RWKV_WIDTH = D_MODEL // 2
RWKV_HEAD_DIM = 64
RWKV_HEADS = RWKV_WIDTH // RWKV_HEAD_DIM
RWKV_DECAY_LORA = 64
RWKV_A_LORA = 64
RWKV_GATE_LORA = 160
RWKV_PROJ = 3 * RWKV_WIDTH + RWKV_DECAY_LORA + RWKV_A_LORA + RWKV_GATE_LORA
RWKV_GN_EPS = 64e-5
RET_HEAD_DIM = 128
RET_WIDTH = D_MODEL // 2
RET_HEADS = RET_WIDTH // RET_HEAD_DIM
RET_PROJ = 4 * RET_WIDTH
RET_GN_EPS = 1e-5
ROPE_BASE = 10000.0
ATT_HEAD_DIM = 64
ATT_WIDTH = D_MODEL // 2
ATT_HEADS = ATT_WIDTH // ATT_HEAD_DIM
ATT_PROJ = 3 * ATT_WIDTH
ATT_LEFT_CHUNKS = 8
ATT_BAND = (ATT_LEFT_CHUNKS + 1) * CHUNK
ATT_PAST_ROWS = ATT_LEFT_CHUNKS * CHUNK
REL_CLIP = 128
GATE_PROJ = 3 * D_MODEL
IN_PROJ = RWKV_PROJ + RET_PROJ + ATT_PROJ + GATE_PROJ
N_EXPERTS = 64
TOP_K = 8
N_GROUPS = 8
TOPK_GROUPS = 4
EXPERT_DIM = 512
SHARED_DIM = 512
ROUTED_SCALE = 2.5
MOE_BLOCK = 128
DN_ALPHA = (2 * DEPTH) ** 0.25
DN_BETA = (8 * DEPTH) ** -0.25
LN_EPS = 1e-5
NEG_INF = -1e30

kernel_name = 'hybrid_rwkv7_retention_chunkattn_moe_stream_step'


def layer_norm(x, g, b):
    xf = x.astype(jnp.float32)
    mu = jnp.mean(xf, -1, keepdims=True)
    var = jnp.mean(jnp.square(xf - mu), -1, keepdims=True)
    return ((xf - mu) * lax.rsqrt(var + LN_EPS) * g + b).astype(x.dtype)


def head_norm(y, g, b, eps):
    mu = jnp.mean(y, -1, keepdims=True)
    var = jnp.mean(jnp.square(y - mu), -1, keepdims=True)
    yn = ((y - mu) * lax.rsqrt(var + eps)).reshape(y.shape[:-2] + (-1,))
    return yn * g + b


def rotary(x, pos):
    half = x.shape[-1] // 2
    inv = ROPE_BASE ** (-jnp.arange(half, dtype=jnp.float32) / half)
    ang = pos.astype(jnp.float32)[:, None] * inv[None, :]
    cos = jnp.cos(ang)[None, :, None, :]
    sin = jnp.sin(ang)[None, :, None, :]
    x1, x2 = x[..., :half], x[..., half:]
    return jnp.concatenate([x1 * cos - x2 * sin, x1 * sin + x2 * cos], -1)


def retention_log_gamma():
    return jnp.log1p(-jnp.exp2(-5.0 - jnp.arange(RET_HEADS, dtype=jnp.float32)))


def rwkv7_scan(r, w, k, v, a, b, s0):
    def step(s, inp):
        r_t, w_t, k_t, v_t, a_t, b_t = inp
        sa = jnp.einsum('bhvk,bhk->bhv', s, a_t)
        s = s * w_t[:, :, None, :] + sa[..., None] * b_t[:, :, None, :] + v_t[..., None] * k_t[:, :, None, :]
        return s, jnp.einsum('bhvk,bhk->bhv', s, r_t)
    xs = tuple(jnp.moveaxis(t, 1, 0) for t in (r, w, k, v, a, b))
    s_final, ys = lax.scan(step, s0, xs)
    return jnp.moveaxis(ys, 0, 1), s_final


def rwkv7_branch(p, shift_prev, s0, mu, w0, w2, a0, a2, g2, k_k, k_a, r_k, ln_g, ln_b):
    n_b, n_t, _ = p.shape
    f32 = jnp.float32
    pf = p.astype(f32)
    prev = jnp.concatenate([shift_prev.astype(f32)[:, None, :], pf[:, :-1]], axis=1)
    px = pf + (prev - pf) * mu
    c = RWKV_WIDTH
    o_w = 3 * c
    o_a = o_w + RWKV_DECAY_LORA
    o_g = o_a + RWKV_A_LORA
    r, k, v = px[..., :c], px[..., c:2 * c], px[..., 2 * c:o_w]
    xw, xa, xg = px[..., o_w:o_a], px[..., o_a:o_g], px[..., o_g:]
    w = -jax.nn.softplus(-(w0 + jnp.tanh(xw) @ w2)) - 0.5
    a = jax.nn.sigmoid(a0 + xa @ a2)
    g = jax.nn.sigmoid(xg) @ g2
    heads = lambda t: t.reshape(n_b, n_t, RWKV_HEADS, RWKV_HEAD_DIM)
    kk = heads(k * k_k)
    kk = kk / jnp.maximum(jnp.sqrt(jnp.sum(kk * kk, -1, keepdims=True)), 1e-12)
    k = k * (1.0 + (a - 1.0) * k_a)
    r_h, k_h, v_h, a_h = heads(r), heads(k), heads(v), heads(a)
    y, s = rwkv7_scan(r_h, heads(jnp.exp(-jnp.exp(w))), k_h, v_h, -kk, kk * a_h, s0.astype(f32))
    y = head_norm(y, ln_g, ln_b, RWKV_GN_EPS)
    y = y + (jnp.sum(r_h * k_h * r_k, -1, keepdims=True) * v_h).reshape(n_b, n_t, c)
    return y * g, s


def retention_chunkwise(q, k, v, s0, log_gamma):
    n_b, n_t, n_h, d_k = q.shape
    d_v = v.shape[-1]
    cs = min(CHUNK, n_t)
    n_c = n_t // cs
    qc = q.reshape(n_b, n_c, cs, n_h, d_k)
    kc = k.reshape(n_b, n_c, cs, n_h, d_k)
    vc = v.reshape(n_b, n_c, cs, n_h, d_v)
    idx = jnp.arange(cs, dtype=jnp.float32)
    diff = idx[:, None] - idx[None, :]
    decay = jnp.where(diff >= 0, jnp.exp(log_gamma[:, None, None] * jnp.maximum(diff, 0.0)), 0.0)
    scores = jnp.einsum('bnihd,bnjhd->bnhij', qc, kc) * decay
    inner = jnp.einsum('bnhij,bnjhv->bnihv', scores, vc)
    k_decay = jnp.exp(log_gamma[None, :] * (cs - 1.0 - idx)[:, None])
    kv = jnp.einsum('bnjhd,jh,bnjhv->bnhdv', kc, k_decay, vc)
    chunk_decay = jnp.exp(log_gamma * cs)[None, :, None, None]

    def step(s, kv_c):
        return s * chunk_decay + kv_c, s
    s_final, s_before = lax.scan(step, s0, jnp.moveaxis(kv, 1, 0))
    q_decay = jnp.exp(log_gamma[None, :] * (idx + 1.0)[:, None])
    cross = jnp.einsum('bnihd,ih,nbhdv->bnihv', qc, q_decay, s_before)
    return (inner + cross).reshape(n_b, n_t, n_h, d_v), s_final


def retention_branch(p, pos, s0, ln_g, ln_b):
    n_b, n_t, _ = p.shape
    heads = lambda t: t.astype(jnp.float32).reshape(n_b, n_t, RET_HEADS, RET_HEAD_DIM)
    q, k, v, g = jnp.split(p, 4, axis=-1)
    qh = rotary(heads(q), pos)
    kh = rotary(heads(k), pos) * (RET_HEAD_DIM ** -0.5)
    y, s = retention_chunkwise(qh, kh, heads(v), s0.astype(jnp.float32), retention_log_gamma())
    y = head_norm(y, ln_g, ln_b, RET_GN_EPS)
    return jax.nn.silu(g.astype(jnp.float32)) * y, s


def rel_bias_lookup(table, dist):
    return table[:, jnp.clip(dist, -REL_CLIP, REL_CLIP) + REL_CLIP].astype(jnp.float32)


def band_attention_prompt(q, k, v, table):
    n_b, n_s, n_h, d = q.shape
    n_c = n_s // CHUNK
    lc = ATT_LEFT_CHUNKS
    zpad = jnp.zeros((n_b, lc * CHUNK, n_h, d), k.dtype)
    kb = jnp.concatenate([zpad, k], 1).reshape(n_b, n_c + lc, CHUNK, n_h, d)
    vb = jnp.concatenate([zpad, v], 1).reshape(n_b, n_c + lc, CHUNK, n_h, d)
    band = jnp.arange(n_c)[:, None] + jnp.arange(lc + 1)[None, :]
    k_band = kb[:, band].reshape(n_b, n_c, ATT_BAND, n_h, d)
    v_band = vb[:, band].reshape(n_b, n_c, ATT_BAND, n_h, d)
    qc = q.reshape(n_b, n_c, CHUNK, n_h, d)
    key_off = jnp.arange(ATT_BAND) - lc * CHUNK
    bias = rel_bias_lookup(table, jnp.arange(CHUNK)[:, None] - key_off[None, :])
    valid = (jnp.arange(n_c)[:, None] * CHUNK + key_off[None, :]) >= 0
    s = jnp.einsum('bnqhd,bnkhd->bnhqk', qc, k_band).astype(jnp.float32) * (d ** -0.5) + bias
    s = jnp.where(valid[None, :, None, None, :], s, NEG_INF)
    pr = jax.nn.softmax(s, axis=-1).astype(v.dtype)
    o = jnp.einsum('bnhqk,bnkhd->bnqhd', pr, v_band)
    return o.reshape(n_b, n_s, n_h * d)


def band_attention_sample(q, k_all, v_all, table):
    n_b, n_t, n_h, d = q.shape
    l_c = k_all.shape[1] - n_t
    dist = (l_c + jnp.arange(n_t))[:, None] - jnp.arange(l_c + n_t)[None, :]
    bias = rel_bias_lookup(table, dist)
    s = jnp.einsum('bqhd,bkhd->bhqk', q, k_all).astype(jnp.float32) * (d ** -0.5) + bias
    pr = jax.nn.softmax(s, axis=-1).astype(v_all.dtype)
    o = jnp.einsum('bhqk,bkhd->bqhd', pr, v_all)
    return o.reshape(n_b, n_t, n_h * d)


def time_mix(x, pos, shift_prev, rwkv_s0, ret_s0, k_cache, v_cache, lw):
    n_b, n_t, _ = x.shape
    h = x @ lw['w_in']
    o1 = RWKV_PROJ
    o2 = o1 + RET_PROJ
    o3 = o2 + ATT_PROJ
    p_rwkv, p_ret, p_att, p_gate = h[..., :o1], h[..., o1:o2], h[..., o2:o3], h[..., o3:]
    y_a, rwkv_s = rwkv7_branch(p_rwkv, shift_prev, rwkv_s0, lw['rwkv_mu'], lw['rwkv_w0'], lw['rwkv_w2'],
                               lw['rwkv_a0'], lw['rwkv_a2'], lw['rwkv_g2'], lw['rwkv_k_k'], lw['rwkv_k_a'],
                               lw['rwkv_r_k'], lw['rwkv_ln_g'], lw['rwkv_ln_b'])
    y_r, ret_s = retention_branch(p_ret, pos, ret_s0, lw['ret_ln_g'], lw['ret_ln_b'])
    q, k, v = [t.reshape(n_b, n_t, ATT_HEADS, ATT_HEAD_DIM) for t in jnp.split(p_att, 3, axis=-1)]
    if k_cache is None:
        y_c = band_attention_prompt(q, k, v, lw['att_rel_bias'])
        keep = min(ATT_PAST_ROWS, n_t)
        k_new, v_new = k[:, n_t - keep:], v[:, n_t - keep:]
    else:
        k_all = jnp.concatenate([k_cache.astype(k.dtype), k], axis=1)
        v_all = jnp.concatenate([v_cache.astype(v.dtype), v], axis=1)
        y_c = band_attention_sample(q, k_all, v_all, lw['att_rel_bias'])
        k_new, v_new = k, v
    g_a, g_r, g_c = jnp.split(jax.nn.sigmoid(p_gate.astype(jnp.float32)), 3, axis=-1)
    dt = x.dtype
    merged = (g_a * (y_a.astype(dt) @ lw['w_branch_rwkv'])
              + g_r * (y_r.astype(dt) @ lw['w_branch_ret'])
              + g_c * (y_c.astype(dt) @ lw['w_branch_att']))
    return merged.astype(dt) @ lw['w_out'], p_rwkv[:, -1], rwkv_s, ret_s, k_new, v_new


def swiglu(x, w_gu, w_dn):
    gt, up = jnp.split(x @ w_gu, 2, axis=-1)
    return (jax.nn.silu(gt) * up) @ w_dn


def route(x, router_w, router_b):
    n_tok = x.shape[0]
    per_group = N_EXPERTS // N_GROUPS
    scores = jax.nn.sigmoid((x @ router_w).astype(jnp.float32))
    sel = scores + router_b.astype(jnp.float32)
    grp_score = jnp.sum(lax.top_k(sel.reshape(n_tok, N_GROUPS, per_group), 2)[0], -1)
    _, grp_idx = lax.top_k(grp_score, TOPK_GROUPS)
    grp_mask = jnp.sum(jax.nn.one_hot(grp_idx, N_GROUPS, dtype=jnp.float32), -2) > 0
    sel = jnp.where(jnp.repeat(grp_mask, per_group, axis=-1), sel, NEG_INF)
    _, idx = lax.top_k(sel, TOP_K)
    w = jnp.take_along_axis(scores, idx, -1)
    return idx, w / jnp.sum(w, -1, keepdims=True) * ROUTED_SCALE


def routed_experts(x, idx, gate, w_gu, w_dn):
    n_tok, d = x.shape
    n_a = n_tok * TOP_K
    n_blocks = -(-n_a // MOE_BLOCK) + N_EXPERTS
    e_flat = idx.reshape(n_a)
    tok = jnp.arange(n_a, dtype=jnp.int32) // TOP_K
    order = jnp.argsort(e_flat)
    e_sorted = e_flat[order]
    counts = jnp.bincount(e_flat, length=N_EXPERTS)
    padded = (counts + MOE_BLOCK - 1) // MOE_BLOCK * MOE_BLOCK
    pad_end = jnp.cumsum(padded)
    pad_start = pad_end - padded
    start = jnp.cumsum(counts) - counts
    slot = (pad_start[e_sorted] + jnp.arange(n_a) - start[e_sorted]).astype(jnp.int32)
    slot_tok = jnp.full((n_blocks * MOE_BLOCK,), n_tok, jnp.int32).at[slot].set(tok[order])
    block_expert = jnp.minimum(jnp.searchsorted(pad_end, jnp.arange(n_blocks) * MOE_BLOCK, side='right'),
                               N_EXPERTS - 1)
    x_pad = jnp.concatenate([x, jnp.zeros((1, d), x.dtype)], 0)
    xb = x_pad[slot_tok].reshape(n_blocks, MOE_BLOCK, d)

    def expert_block(args):
        xblk, e = args
        return swiglu(xblk, w_gu[e], w_dn[e])
    yb = lax.map(expert_block, (xb, block_expert)).reshape(n_blocks * MOE_BLOCK, d)
    slot_orig = jnp.zeros((n_a,), jnp.int32).at[order].set(slot)
    y_assign = yb[slot_orig].reshape(n_tok, TOP_K, d)
    return jnp.einsum('tkd,tk->td', y_assign, gate.astype(y_assign.dtype))


def moe_ffn(x, lw):
    def per_row(xr):
        idx, gate = route(xr, lw['router_w'], lw['router_bias'])
        return (routed_experts(xr, idx, gate, lw['expert_w_gate_up'], lw['expert_w_down'])
                + swiglu(xr, lw['shared_w_gate_up'], lw['shared_w_down']))
    return lax.map(per_row, x)


def run_layer(x, pos, shift_prev, rwkv_s0, ret_s0, k_cache, v_cache, lw):
    mix, sh, rs, ts, kn, vn = time_mix(x, pos, shift_prev, rwkv_s0, ret_s0, k_cache, v_cache, lw)
    x = layer_norm(DN_ALPHA * x + mix, lw['ln1_g'], lw['ln1_b'])
    x = layer_norm(DN_ALPHA * x + moe_ffn(x, lw), lw['ln2_g'], lw['ln2_b'])
    return x, (kn, vn, rs, sh, ts)


def setup_inputs(seed: int = 0) -> dict:
    key = jax.random.key(seed)
    ks = iter(jax.random.split(key, 48))
    f32 = jnp.float32

    def nrm(shape, scale):
        return jax.random.normal(next(ks), shape, f32) * scale

    def unif(shape, lo, hi):
        return jax.random.uniform(next(ks), shape, f32, lo, hi)

    l_c = min(ATT_PAST_ROWS, PAST_LEN)
    c = RWKV_WIDTH
    return {
        'x_prompt': nrm((BATCH, SEQ, D_MODEL), 1.0),
        'x_sample': nrm((DEC_BATCH, DEC_SEQ, D_MODEL), 1.0),
        'cache_attn_k': nrm((DEPTH, DEC_BATCH, l_c, ATT_HEADS, ATT_HEAD_DIM), 1.0),
        'cache_attn_v': nrm((DEPTH, DEC_BATCH, l_c, ATT_HEADS, ATT_HEAD_DIM), 1.0),
        'state_rwkv': nrm((DEPTH, DEC_BATCH, RWKV_HEADS, RWKV_HEAD_DIM, RWKV_HEAD_DIM), 0.3),
        'state_rwkv_shift': nrm((DEPTH, DEC_BATCH, RWKV_PROJ), 1.0),
        'state_ret': nrm((DEPTH, DEC_BATCH, RET_HEADS, RET_HEAD_DIM, RET_HEAD_DIM), 0.5),
        'w_in': nrm((DEPTH, D_MODEL, IN_PROJ), D_MODEL ** -0.5),
        'rwkv_mu': unif((DEPTH, RWKV_PROJ), 0.0, 1.0),
        'rwkv_w0': unif((DEPTH, c), -6.0, -1.0),
        'rwkv_w2': nrm((DEPTH, RWKV_DECAY_LORA, c), 0.1 * RWKV_DECAY_LORA ** -0.5),
        'rwkv_a0': nrm((DEPTH, c), 0.1),
        'rwkv_a2': nrm((DEPTH, RWKV_A_LORA, c), 0.1 * RWKV_A_LORA ** -0.5),
        'rwkv_g2': nrm((DEPTH, RWKV_GATE_LORA, c), RWKV_GATE_LORA ** -0.5),
        'rwkv_k_k': 0.85 + nrm((DEPTH, c), 0.02),
        'rwkv_k_a': 1.0 + nrm((DEPTH, c), 0.02),
        'rwkv_r_k': nrm((DEPTH, RWKV_HEADS, RWKV_HEAD_DIM), 0.1),
        'rwkv_ln_g': 1.0 + nrm((DEPTH, c), 0.02),
        'rwkv_ln_b': nrm((DEPTH, c), 0.02),
        'ret_ln_g': 1.0 + nrm((DEPTH, RET_WIDTH), 0.02),
        'ret_ln_b': nrm((DEPTH, RET_WIDTH), 0.02),
        'att_rel_bias': nrm((DEPTH, ATT_HEADS, 2 * REL_CLIP + 1), 0.1),
        'w_branch_rwkv': nrm((DEPTH, c, D_MODEL), c ** -0.5),
        'w_branch_ret': nrm((DEPTH, RET_WIDTH, D_MODEL), RET_WIDTH ** -0.5),
        'w_branch_att': nrm((DEPTH, ATT_WIDTH, D_MODEL), ATT_WIDTH ** -0.5),
        'w_out': nrm((DEPTH, D_MODEL, D_MODEL), D_MODEL ** -0.5 * DN_BETA),
        'ln1_g': 1.0 + nrm((DEPTH, D_MODEL), 0.02),
        'ln1_b': nrm((DEPTH, D_MODEL), 0.02),
        'router_w': nrm((DEPTH, D_MODEL, N_EXPERTS), D_MODEL ** -0.5),
        'router_bias': nrm((DEPTH, N_EXPERTS), 0.01),
        'expert_w_gate_up': nrm((DEPTH, N_EXPERTS, D_MODEL, 2 * EXPERT_DIM), D_MODEL ** -0.5),
        'expert_w_down': nrm((DEPTH, N_EXPERTS, EXPERT_DIM, D_MODEL), EXPERT_DIM ** -0.5 * DN_BETA),
        'shared_w_gate_up': nrm((DEPTH, D_MODEL, 2 * SHARED_DIM), D_MODEL ** -0.5),
        'shared_w_down': nrm((DEPTH, SHARED_DIM, D_MODEL), SHARED_DIM ** -0.5 * DN_BETA),
        'ln2_g': 1.0 + nrm((DEPTH, D_MODEL), 0.02),
        'ln2_b': nrm((DEPTH, D_MODEL), 0.02),
    }


def reference(x_prompt, x_sample, cache_attn_k, cache_attn_v, state_rwkv, state_rwkv_shift, state_ret,
              w_in, rwkv_mu, rwkv_w0, rwkv_w2, rwkv_a0, rwkv_a2, rwkv_g2, rwkv_k_k, rwkv_k_a, rwkv_r_k,
              rwkv_ln_g, rwkv_ln_b, ret_ln_g, ret_ln_b, att_rel_bias, w_branch_rwkv, w_branch_ret,
              w_branch_att, w_out, ln1_g, ln1_b, router_w, router_bias, expert_w_gate_up, expert_w_down,
              shared_w_gate_up, shared_w_down, ln2_g, ln2_b):
    f32 = jnp.float32
    n_bp, n_s, _ = x_prompt.shape
    n_t = x_sample.shape[1]
    pos_p = jnp.arange(n_s, dtype=jnp.int32)
    pos_s = PAST_LEN + jnp.arange(n_t, dtype=jnp.int32)
    yp, ys = x_prompt, x_sample
    new_p = [[], [], [], [], []]
    new_s = [[], [], [], [], []]
    for l in range(DEPTH):
        lw = {
            'w_in': w_in[l], 'rwkv_mu': rwkv_mu[l], 'rwkv_w0': rwkv_w0[l], 'rwkv_w2': rwkv_w2[l],
            'rwkv_a0': rwkv_a0[l], 'rwkv_a2': rwkv_a2[l], 'rwkv_g2': rwkv_g2[l], 'rwkv_k_k': rwkv_k_k[l],
            'rwkv_k_a': rwkv_k_a[l], 'rwkv_r_k': rwkv_r_k[l], 'rwkv_ln_g': rwkv_ln_g[l],
            'rwkv_ln_b': rwkv_ln_b[l], 'ret_ln_g': ret_ln_g[l], 'ret_ln_b': ret_ln_b[l],
            'att_rel_bias': att_rel_bias[l], 'w_branch_rwkv': w_branch_rwkv[l],
            'w_branch_ret': w_branch_ret[l], 'w_branch_att': w_branch_att[l], 'w_out': w_out[l],
            'ln1_g': ln1_g[l], 'ln1_b': ln1_b[l], 'router_w': router_w[l], 'router_bias': router_bias[l],
            'expert_w_gate_up': expert_w_gate_up[l], 'expert_w_down': expert_w_down[l],
            'shared_w_gate_up': shared_w_gate_up[l], 'shared_w_down': shared_w_down[l],
            'ln2_g': ln2_g[l], 'ln2_b': ln2_b[l],
        }
        yp, st_p = run_layer(yp, pos_p,
                             jnp.zeros((n_bp, RWKV_PROJ), yp.dtype),
                             jnp.zeros((n_bp, RWKV_HEADS, RWKV_HEAD_DIM, RWKV_HEAD_DIM), f32),
                             jnp.zeros((n_bp, RET_HEADS, RET_HEAD_DIM, RET_HEAD_DIM), f32),
                             None, None, lw)
        ys, st_s = run_layer(ys, pos_s, state_rwkv_shift[l], state_rwkv[l], state_ret[l],
                             cache_attn_k[l], cache_attn_v[l], lw)
        for lst, arr in zip(new_p, st_p):
            lst.append(arr)
        for lst, arr in zip(new_s, st_s):
            lst.append(arr)
    p_attn_k, p_attn_v, p_rwkv, p_rwkv_shift, p_ret = [jnp.stack(a, 0) for a in new_p]
    s_attn_k, s_attn_v, s_rwkv, s_rwkv_shift, s_ret = [jnp.stack(a, 0) for a in new_s]
    return (yp, ys, p_attn_k, p_attn_v, p_rwkv, p_rwkv_shift, p_ret,
            s_attn_k, s_attn_v, s_rwkv, s_rwkv_shift, s_ret)
```

```python
import functools
import math

import jax
import jax.numpy as jnp
import numpy as np
from jax import lax
from jax.experimental import pallas as pl
from jax.experimental.pallas import tpu as pltpu

F32 = jnp.float32
BF16 = jnp.bfloat16

D_MODEL = 2048
PAST_LEN = 1024
CHUNK = 64
RWKV_WIDTH = 1024
RWKV_HEAD_DIM = 64
RWKV_HEADS = 16
RWKV_DECAY_LORA = 64
RWKV_A_LORA = 64
RWKV_GATE_LORA = 160
RWKV_PROJ = 3 * RWKV_WIDTH + RWKV_DECAY_LORA + RWKV_A_LORA + RWKV_GATE_LORA
RWKV_PROJ_PAD = 3584
RWKV_GN_EPS = 64e-5
RET_HEAD_DIM = 128
RET_WIDTH = 1024
RET_HEADS = 8
RET_GN_EPS = 1e-5
ROPE_BASE = 10000.0
RET_CHUNK = 64
ATT_HEAD_DIM = 64
ATT_WIDTH = 1024
ATT_HEADS = 16
ATT_LEFT_CHUNKS = 8
ATT_PAST_ROWS = ATT_LEFT_CHUNKS * CHUNK
ATT_QBLK = 256
ATT_WIN = ATT_QBLK + ATT_PAST_ROWS
REL_CLIP = 128
OFF_RET = RWKV_PROJ_PAD
OFF_ATT = OFF_RET + 4 * RET_WIDTH
OFF_GATE = OFF_ATT + 3 * ATT_WIDTH
IN_PROJ_PAD = OFF_GATE + 3 * D_MODEL
N_EXPERTS = 64
TOP_K = 8
N_GROUPS = 8
TOPK_GROUPS = 4
EXPERT_DIM = 512
SHARED_DIM = 512
ROUTED_SCALE = 2.5
MOE_BLOCK = 512
LN_EPS = 1e-5
NEG_INF = -1e30
LANES = 128

VMEM_LIMIT = 56 * 1024 * 1024


def _cparams(sem):
    return pltpu.CompilerParams(dimension_semantics=sem, vmem_limit_bytes=VMEM_LIMIT)


def _tile(n, target, align=8):
    for t in range(min(n, target), 0, -1):
        if n % t == 0 and t % align == 0:
            return t
    return n


def _dot(a, b):
    return jnp.dot(a, b, preferred_element_type=F32)


def _dot_nt(a, b):
    return lax.dot_general(a, b, (((1,), (1,)), ((), ())), preferred_element_type=F32)


def _dot_tn(a, b):
    return lax.dot_general(a, b, (((0,), (0,)), ((), ())), preferred_element_type=F32)


def _sigmoid(x):
    return 1.0 / (1.0 + jnp.exp(-x))


def _split3(x):
    hi = x.astype(BF16)
    r1 = x - hi.astype(F32)
    mid = r1.astype(BF16)
    lo = (r1 - mid.astype(F32)).astype(BF16)
    return hi, mid, lo


def _mm_kernel(x_ref, w_ref, o_ref):
    o_ref[...] = _dot(x_ref[...], w_ref[...])


def _matmul(x, w, tm_target=1280, tn_target=768):
    m, k = x.shape
    n = w.shape[1]
    tm = _tile(m, tm_target)
    tn = _tile(n, tn_target, LANES)
    return pl.pallas_call(
        _mm_kernel,
        out_shape=jax.ShapeDtypeStruct((m, n), F32),
        grid=(m // tm, n // tn),
        in_specs=[pl.BlockSpec((tm, k), lambda i, j: (i, 0)),
                  pl.BlockSpec((k, tn), lambda i, j: (0, j))],
        out_specs=pl.BlockSpec((tm, tn), lambda i, j: (i, j)),
        compiler_params=_cparams(("parallel", "parallel")),
        name="in_proj",
    )(x, w)


def _rwkv_kernel(*refs, n_rows):
    p_refs = refs[:n_rows]
    (shift_ref, s0_ref, mu_ref, w0_ref, w2_ref, a0_ref, a2_ref, g2_ref, kk_ref, ka_ref, rk_ref,
     lng_ref, lnb_ref, e_ref, et_ref, ones2_ref, ones3_ref,
     y_ref, sout_ref, shout_ref,
     s_scr, sb_scr, prev_scr, ar_scr, r_scr, w_scr, b_scr, k_scr, v1_scr, v2_scr, v3_scr,
     yo_scr, g_scr) = refs[n_rows:]
    c = pl.program_id(1)
    n_c = pl.num_programs(1)
    C = p_refs[0].shape[0]
    n_pairs = RWKV_WIDTH // LANES
    hd = RWKV_HEAD_DIM

    @pl.when(c == 0)
    def _():
        s_scr[...] = s0_ref[...]
        sb_scr[...] = s0_ref[...].astype(BF16)
        prev_scr[...] = shift_ref[...]

    e_m = e_ref[...]
    et_m = et_ref[...]

    def headsum(x):
        s = None
        for limb in _split3(x):
            t = _dot(limb, e_m)
            s = t if s is None else s + t
        out = None
        for limb in _split3(s):
            t = _dot(limb, et_m)
            out = t if out is None else out + t
        return out

    w = RWKV_WIDTH
    for rr in range(n_rows):
        pf = p_refs[rr][...]
        row = lax.broadcasted_iota(jnp.int32, pf.shape, 0)
        prev = jnp.where(row == 0, prev_scr[rr, 0:1, :], pltpu.roll(pf, 1, 0))
        prev_scr[rr, 0:1, :] = pf[C - 1:C, :]
        px = pf + (prev - pf) * mu_ref[...]
        r = px[:, 0:w]
        k = px[:, w:2 * w]
        v = px[:, 2 * w:3 * w]
        lora = px[:, 3 * w:3 * w + LANES]
        xg = px[:, 3 * w + LANES:RWKV_PROJ_PAD]
        z = w0_ref[...] + _dot(jnp.tanh(lora).astype(BF16), w2_ref[...])
        w_log = -(jnp.maximum(-z, 0.0) + jnp.log1p(jnp.exp(-jnp.abs(z)))) - 0.5
        a = _sigmoid(a0_ref[...] + _dot(lora.astype(BF16), a2_ref[...]))
        g_scr[rr] = _dot(_sigmoid(xg).astype(BF16), g2_ref[...])
        kk = k * kk_ref[...]
        kk = kk / jnp.maximum(jnp.sqrt(headsum(kk * kk)), 1e-12)
        ar_scr[rr] = (-kk).astype(BF16).astype(F32)
        r_scr[rr] = r
        w_scr[rr] = jnp.exp(-jnp.exp(w_log))
        b_scr[rr] = kk * a
        k_scr[rr] = k * (1.0 + (a - 1.0) * ka_ref[...])
        v1, v2, v3 = _split3(v)
        v1_scr[rr] = v1.astype(F32)
        v2_scr[rr] = v2.astype(F32)
        v3_scr[rr] = v3.astype(F32)

    lane = lax.broadcasted_iota(jnp.int32, (hd, LANES), 1)
    sub = lax.broadcasted_iota(jnp.int32, (hd, LANES), 0)
    mdiag = (lane & (hd - 1)) == sub
    lane8 = lax.broadcasted_iota(jnp.int32, (8, LANES), 1)
    sub8 = lax.broadcasted_iota(jnp.int32, (8, LANES), 0)
    hsel = (lane8 // hd) == sub8
    ones2 = ones2_ref[...]
    ones3 = ones3_ref[...]
    pairs = [(rr, p) for rr in range(n_rows) for p in range(n_pairs)]
    yo_scr[...] = jnp.zeros(yo_scr.shape, F32)

    def step(t, carry):
        t0 = pl.multiple_of((t // 8) * 8, 8)
        tj = t - t0
        to_top = (8 - tj) & 7

        def row(scr, rr, ls):
            return pltpu.roll(scr[rr, pl.ds(t0, 8), ls], to_top, 0)[0:1, :]

        hl, vd = [], []
        for rr, p in pairs:
            ls = slice(p * LANES, (p + 1) * LANES)
            prod = sb_scr[rr, p].astype(F32) * row(ar_scr, rr, ls)
            bits = lax.bitcast_convert_type(prod, jnp.uint32) & jnp.uint32(0xFFFF0000)
            hi = lax.bitcast_convert_type(bits, F32)
            hl.append(jnp.concatenate([hi.astype(BF16), (prod - hi).astype(BF16)], axis=1))
            vd.append(jnp.concatenate([jnp.where(mdiag, row(v1_scr, rr, ls), 0.0),
                                       jnp.where(mdiag, row(v2_scr, rr, ls), 0.0),
                                       jnp.where(mdiag, row(v3_scr, rr, ls), 0.0)],
                                      axis=1).astype(BF16))
        sa_all = _dot(jnp.concatenate(hl, axis=0), ones2)
        vb_all = _dot(jnp.concatenate(vd, axis=0), ones3)
        for i, (rr, p) in enumerate(pairs):
            ls = slice(p * LANES, (p + 1) * LANES)
            rows = slice(i * hd, (i + 1) * hd)
            s_new = ((s_scr[rr, p] * row(w_scr, rr, ls) + sa_all[rows] * row(b_scr, rr, ls))
                     + vb_all[rows] * row(k_scr, rr, ls))
            s_scr[rr, p] = s_new
            sb = s_new.astype(BF16)
            sb_scr[rr, p] = sb
            r_rows = jnp.where(hsel, row(r_scr, rr, ls), 0.0).astype(BF16)
            y2 = _dot_nt(r_rows, sb)
            y_row = jnp.concatenate([y2[0:1, :], y2[1:2, :]], axis=1)
            yo_scr[rr, pl.ds(t0, 8), ls] = jnp.where(sub8 == tj, y_row, yo_scr[rr, pl.ds(t0, 8), ls])
        return carry

    lax.fori_loop(0, C, step, 0)

    inv_n = 1.0 / hd
    for rr in range(n_rows):
        y = yo_scr[rr]
        mean = headsum(y) * inv_n
        d = y - mean
        var = headsum(d * d) * inv_n
        yn = d * lax.rsqrt(var + RWKV_GN_EPS) * lng_ref[...] + lnb_ref[...]
        v = (v1_scr[rr] + v2_scr[rr]) + v3_scr[rr]
        bonus = headsum(r_scr[rr] * k_scr[rr] * rk_ref[...]) * v
        y_ref[rr] = ((yn + bonus) * g_scr[rr]).astype(y_ref.dtype)

    @pl.when(c == n_c - 1)
    def _():
        sout_ref[...] = s_scr[...]
        shout_ref[...] = prev_scr[...]


def _rwkv_consts():
    lane = np.arange(RWKV_WIDTH)
    e = (lane[:, None] // RWKV_HEAD_DIM == np.arange(LANES)[None, :]).astype(np.float32)
    gl = np.arange(LANES) // RWKV_HEAD_DIM
    ones = (gl[:, None] == gl[None, :]).astype(np.float32)
    return (jnp.asarray(e, BF16), jnp.asarray(e.T, BF16),
            jnp.asarray(np.concatenate([ones] * 2, 0), BF16), jnp.asarray(np.concatenate([ones] * 3, 0), BF16))


def _rwkv_call(h, row_blk0, n_b, n_t, shift0, s0_pairs, lp):
    C = CHUNK
    n_c = n_t // C
    n_rows = 2 if n_b % 2 == 0 else 1
    n_pairs = RWKV_WIDTH // LANES
    consts = _rwkv_consts()
    full = lambda arr: pl.BlockSpec(arr.shape, lambda i, c: (0,) * arr.ndim)
    params = [lp['rwkv_mu'], lp['rwkv_w0'], lp['rwkv_w2'], lp['rwkv_a0'], lp['rwkv_a2'], lp['rwkv_g2'],
              lp['rwkv_k_k'], lp['rwkv_k_a'], lp['rwkv_r_k'], lp['rwkv_ln_g'], lp['rwkv_ln_b']]

    def pspec(rr):
        return pl.BlockSpec((C, RWKV_PROJ_PAD), lambda i, c: (row_blk0 + (i * n_rows + rr) * n_c + c, 0))
    state_spec = pl.BlockSpec((n_rows, n_pairs, RWKV_HEAD_DIM, LANES), lambda i, c: (i, 0, 0, 0))
    shift_spec = pl.BlockSpec((n_rows, 8, RWKV_PROJ_PAD), lambda i, c: (i, 0, 0))
    rows_scr = pltpu.VMEM((n_rows, C, RWKV_WIDTH), F32)
    y, s_out, sh_out = pl.pallas_call(
        functools.partial(_rwkv_kernel, n_rows=n_rows),
        out_shape=(jax.ShapeDtypeStruct((n_b, n_t, RWKV_WIDTH), BF16),
                   jax.ShapeDtypeStruct((n_b, n_pairs, RWKV_HEAD_DIM, LANES), F32),
                   jax.ShapeDtypeStruct((n_b, 8, RWKV_PROJ_PAD), F32)),
        grid=(n_b // n_rows, n_c),
        in_specs=[pspec(rr) for rr in range(n_rows)] + [shift_spec, state_spec]
                 + [full(a) for a in params] + [full(a) for a in consts],
        out_specs=(pl.BlockSpec((n_rows, C, RWKV_WIDTH), lambda i, c: (i, c, 0)), state_spec, shift_spec),
        scratch_shapes=[pltpu.VMEM((n_rows, n_pairs, RWKV_HEAD_DIM, LANES), F32),
                        pltpu.VMEM((n_rows, n_pairs, RWKV_HEAD_DIM, LANES), BF16),
                        pltpu.VMEM((n_rows, 8, RWKV_PROJ_PAD), F32)] + [rows_scr] * 10,
        compiler_params=_cparams(("parallel", "arbitrary")),
        name="rwkv7_scan",
    )(*([h] * n_rows), shift0, s0_pairs, *params, *consts)
    return y.reshape(n_b * n_t, RWKV_WIDTH), s_out, sh_out


def _rwkv_state_to_pairs(s):
    n_b = s.shape[0]
    s5 = s.astype(F32).reshape(n_b, RWKV_HEADS // 2, 2, RWKV_HEAD_DIM, RWKV_HEAD_DIM)
    return s5.transpose(0, 1, 3, 2, 4).reshape(n_b, RWKV_HEADS // 2, RWKV_HEAD_DIM, LANES)


def _rwkv_state_from_pairs(sp):
    n_b = sp.shape[0]
    s5 = sp.reshape(n_b, RWKV_HEADS // 2, RWKV_HEAD_DIM, 2, RWKV_HEAD_DIM)
    return s5.transpose(0, 1, 3, 2, 4).reshape(n_b, RWKV_HEADS, RWKV_HEAD_DIM, RWKV_HEAD_DIM)


def _ret_kernel(q_ref, k_ref, v_ref, g_ref, cc_ref, ss_ref, dec_ref, qd_ref, kd_ref, s0_ref,
                lng_ref, lnb_ref, y_ref, sout_ref, s_scr, *, chunk_decay):
    hh = pl.program_id(1)
    c = pl.program_id(2)
    n_c = pl.num_programs(2)
    n_h = s_scr.shape[0]
    d = RET_HEAD_DIM

    @pl.when(c == 0)
    def _():
        s_scr[...] = s0_ref[0]

    cc = cc_ref[...]
    ss = ss_ref[...]
    outs = []
    for j in range(n_h):
        sl = slice(j * d, (j + 1) * d)
        q = q_ref[:, sl]
        k = k_ref[:, sl]
        v = v_ref[:, sl].astype(BF16)
        qr = q * cc + pltpu.roll(q, d // 2, 1) * ss
        kr = (k * cc + pltpu.roll(k, d // 2, 1) * ss) * (d ** -0.5)
        qb = qr.astype(BF16)
        scores = _dot_nt(qb, kr.astype(BF16)) * dec_ref[j]
        inner = _dot(scores.astype(BF16), v)
        s_j = s_scr[j]
        cross = _dot((qr * qd_ref[j]).astype(BF16), s_j.astype(BF16))
        kv = _dot_tn((kr * kd_ref[j]).astype(BF16), v)
        cd = jnp.where(hh == 0, chunk_decay[j], chunk_decay[n_h + j])
        s_scr[j] = s_j * cd + kv
        y = inner + cross
        mu = jnp.mean(y, axis=-1, keepdims=True)
        yc = y - mu
        var = jnp.mean(yc * yc, axis=-1, keepdims=True)
        outs.append(yc * lax.rsqrt(var + RET_GN_EPS))
    yn = jnp.concatenate(outs, axis=1) * lng_ref[...] + lnb_ref[...]
    gt = g_ref[...]
    y_ref[...] = (gt * _sigmoid(gt) * yn).astype(y_ref.dtype)

    @pl.when(c == n_c - 1)
    def _():
        sout_ref[0] = s_scr[...]


def _ret_consts(C, pos0, n_t):
    f = np.float32
    log_gamma = np.log1p(-np.exp2(-5.0 - np.arange(RET_HEADS, dtype=f))).astype(f)
    idx = np.arange(C, dtype=f)
    diff = idx[:, None] - idx[None, :]
    dec = np.where(diff >= 0, np.exp(log_gamma[:, None, None] * np.maximum(diff, 0.0)), 0.0).astype(f)
    qd = np.exp(log_gamma[:, None] * (idx + 1.0)[None, :]).astype(f)
    kd = np.exp(log_gamma[:, None] * (C - 1.0 - idx)[None, :]).astype(f)
    qd = np.broadcast_to(qd[:, :, None], (RET_HEADS, C, RET_HEAD_DIM)).copy()
    kd = np.broadcast_to(kd[:, :, None], (RET_HEADS, C, RET_HEAD_DIM)).copy()
    cd = tuple(float(x) for x in np.exp(log_gamma * C))
    half = RET_HEAD_DIM // 2
    inv = (ROPE_BASE ** (-np.arange(half, dtype=f) / half)).astype(f)
    pos = (pos0 + np.arange(n_t)).astype(f)
    ang = pos[:, None] * inv[None, :]
    cos, sin = np.cos(ang).astype(f), np.sin(ang).astype(f)
    cc = np.concatenate([cos, cos], axis=1)
    ss = np.concatenate([-sin, sin], axis=1)
    return jnp.asarray(dec), jnp.asarray(qd), jnp.asarray(kd), cd, jnp.asarray(cc), jnp.asarray(ss)


def _ret_call(h, row_blk0, n_b, n_t, pos0, s0, lp):
    C = min(RET_CHUNK, n_t)
    n_c = n_t // C
    hw = 512
    n_hh = RET_WIDTH // hw
    hp = hw // RET_HEAD_DIM
    dec, qd, kd, cd, cc, ss = _ret_consts(C, pos0, n_t)
    col0 = OFF_RET // hw
    blk = lambda part: pl.BlockSpec(
        (C, hw), lambda b, hh, c: (row_blk0 + b * n_c + c, col0 + part * n_hh + hh))
    y, s_out = pl.pallas_call(
        functools.partial(_ret_kernel, chunk_decay=cd),
        out_shape=(jax.ShapeDtypeStruct((n_b * n_t, RET_WIDTH), BF16),
                   jax.ShapeDtypeStruct((n_b, RET_HEADS, RET_HEAD_DIM, RET_HEAD_DIM), F32)),
        grid=(n_b, n_hh, n_c),
        in_specs=[blk(0), blk(1), blk(2), blk(3),
                  pl.BlockSpec((C, RET_HEAD_DIM), lambda b, hh, c: (c, 0)),
                  pl.BlockSpec((C, RET_HEAD_DIM), lambda b, hh, c: (c, 0)),
                  pl.BlockSpec((hp, C, C), lambda b, hh, c: (hh, 0, 0)),
                  pl.BlockSpec((hp, C, RET_HEAD_DIM), lambda b, hh, c: (hh, 0, 0)),
                  pl.BlockSpec((hp, C, RET_HEAD_DIM), lambda b, hh, c: (hh, 0, 0)),
                  pl.BlockSpec((1, hp, RET_HEAD_DIM, RET_HEAD_DIM), lambda b, hh, c: (b, hh, 0, 0)),
                  pl.BlockSpec((1, hw), lambda b, hh, c: (0, hh)),
                  pl.BlockSpec((1, hw), lambda b, hh, c: (0, hh))],
        out_specs=(pl.BlockSpec((C, hw), lambda b, hh, c: (b * n_c + c, hh)),
                   pl.BlockSpec((1, hp, RET_HEAD_DIM, RET_HEAD_DIM), lambda b, hh, c: (b, hh, 0, 0))),
        scratch_shapes=[pltpu.VMEM((hp, RET_HEAD_DIM, RET_HEAD_DIM), F32)],
        compiler_params=_cparams(("parallel", "parallel", "arbitrary")),
        name="retention_chunk",
    )(h, h, h, h, cc, ss, dec, qd, kd, s0.astype(F32), lp['ret_ln_g'], lp['ret_ln_b'])
    return y, s_out


def _att_kernel(q_ref, k0_ref, k1_ref, k2_ref, v0_ref, v1_ref, v2_ref, bias_ref, o_ref, *, blk0):
    blk = pl.program_id(2) + blk0
    hd = ATT_HEAD_DIM
    n_pairs = q_ref.shape[1] // LANES
    lane = lax.broadcasted_iota(jnp.int32, (1, LANES), 1)
    k_refs = (k0_ref, k1_ref, k2_ref)
    v_refs = (v0_ref, v1_ref, v2_ref)
    valid = (blk >= 2, blk >= 1, None)
    outs = []
    for p in range(n_pairs):
        sl = slice(p * LANES, (p + 1) * LANES)
        q = q_ref[:, sl]
        ks = [kr[:, sl].astype(BF16) for kr in k_refs]
        vs = [vr[:, sl].astype(BF16) for vr in v_refs]
        o_heads = []
        for j in range(2):
            qm = jnp.where((lane // hd) == j, q, 0.0).astype(BF16)
            parts = []
            for kb in range(3):
                s = _dot_nt(qm, ks[kb]) * (hd ** -0.5)
                s = s + bias_ref[2 * p + j, :, kb * ATT_QBLK:(kb + 1) * ATT_QBLK]
                if valid[kb] is not None:
                    s = jnp.where(valid[kb], s, NEG_INF)
                parts.append(s)
            s = jnp.concatenate(parts, axis=1)
            m = jnp.max(s, axis=-1, keepdims=True)
            e = jnp.exp(s - m)
            pr = (e / jnp.sum(e, axis=-1, keepdims=True)).astype(BF16)
            o = _dot(pr[:, 0:ATT_QBLK], vs[0])
            o = o + _dot(pr[:, ATT_QBLK:2 * ATT_QBLK], vs[1])
            o = o + _dot(pr[:, 2 * ATT_QBLK:3 * ATT_QBLK], vs[2])
            o_heads.append(o)
        outs.append(jnp.where((lane // hd) == 0, o_heads[0], o_heads[1]))
    o_ref[...] = jnp.concatenate(outs, axis=1).astype(o_ref.dtype)


def _att_bias(table):
    r = np.arange(ATT_QBLK)[:, None]
    w = np.arange(ATT_WIN)[None, :]
    dist = np.clip(r - w + ATT_PAST_ROWS, -REL_CLIP, REL_CLIP) + REL_CLIP
    lo = CHUNK * (r // CHUNK)
    band = (w >= lo) & (w < lo + ATT_PAST_ROWS + CHUNK)
    b = table.astype(F32)[:, dist]
    return jnp.where(jnp.asarray(band)[None], b, NEG_INF)


def _att_call(q_arr, k_arr, v_arr, cols, row_blk0, n_b, n_blk_batch, blk0, n_blk, bias):
    gw = 256
    n_hg = ATT_WIDTH // gw
    hpg = gw // ATT_HEAD_DIM
    qc, kc, vc = cols
    rowb = lambda b, i: row_blk0 + b * n_blk_batch + i
    qspec = pl.BlockSpec((ATT_QBLK, gw), lambda g, b, i: (rowb(b, i + blk0), qc + g))

    def kvspec(col, back):
        return pl.BlockSpec((ATT_QBLK, gw),
                            lambda g, b, i: (rowb(b, jnp.maximum(i + blk0 - back, 0)), col + g))
    return pl.pallas_call(
        functools.partial(_att_kernel, blk0=blk0),
        out_shape=jax.ShapeDtypeStruct((n_b * n_blk * ATT_QBLK, ATT_WIDTH), BF16),
        grid=(n_hg, n_b, n_blk),
        in_specs=[qspec, kvspec(kc, 2), kvspec(kc, 1), kvspec(kc, 0),
                  kvspec(vc, 2), kvspec(vc, 1), kvspec(vc, 0),
                  pl.BlockSpec((hpg, ATT_QBLK, ATT_WIN), lambda g, b, i: (g, 0, 0))],
        out_specs=pl.BlockSpec((ATT_QBLK, gw), lambda g, b, i: (b * n_blk + i, g)),
        compiler_params=_cparams(("parallel", "parallel", "arbitrary")),
        name="band_attention",
    )(q_arr, k_arr, k_arr, k_arr, v_arr, v_arr, v_arr, bias)


def _merge_kernel(ya_ref, yr_ref, yc_ref, ga_ref, gr_ref, gc_ref, wa_ref, wr_ref, wc_ref, o_ref):
    m = _sigmoid(ga_ref[...]) * _dot(ya_ref[...], wa_ref[...])
    m = m + _sigmoid(gr_ref[...]) * _dot(yr_ref[...], wr_ref[...])
    m = m + _sigmoid(gc_ref[...]) * _dot(yc_ref[...], wc_ref[...])
    o_ref[...] = m.astype(o_ref.dtype)


def _merge_call(ya, yr, yc, h, lp):
    m = ya.shape[0]
    tm = _tile(m, 512)
    tn = 512
    n_n = D_MODEL // tn
    g0 = OFF_GATE // tn
    yspec = pl.BlockSpec((tm, RWKV_WIDTH), lambda i, j: (i, 0))
    wspec = pl.BlockSpec((RWKV_WIDTH, tn), lambda i, j: (0, j))
    gspec = lambda part: pl.BlockSpec((tm, tn), lambda i, j: (i, g0 + part * n_n + j))
    return pl.pallas_call(
        _merge_kernel,
        out_shape=jax.ShapeDtypeStruct((m, D_MODEL), BF16),
        grid=(m // tm, n_n),
        in_specs=[yspec, yspec, yspec, gspec(0), gspec(1), gspec(2), wspec, wspec, wspec],
        out_specs=pl.BlockSpec((tm, tn), lambda i, j: (i, j)),
        compiler_params=_cparams(("parallel", "parallel")),
        name="branch_merge",
    )(ya, yr, yc, h, h, h, lp['w_branch_rwkv'], lp['w_branch_ret'], lp['w_branch_att'])


def _layer_norm(z, g, b):
    mu = jnp.mean(z, axis=-1, keepdims=True)
    zc = z - mu
    var = jnp.mean(zc * zc, axis=-1, keepdims=True)
    return zc * lax.rsqrt(var + LN_EPS) * g + b


def _outproj_kernel(x_ref, m_ref, w_ref, g_ref, b_ref, o_ref, ob_ref, *, alpha):
    z = alpha * x_ref[...] + _dot(m_ref[...], w_ref[...])
    y = _layer_norm(z, g_ref[...], b_ref[...])
    o_ref[...] = y
    ob_ref[...] = y.astype(ob_ref.dtype)


def _outproj_call(x, merged, lp, alpha):
    m = x.shape[0]
    tm = _tile(m, 256)
    row = pl.BlockSpec((tm, D_MODEL), lambda i: (i, 0))
    vec = pl.BlockSpec((1, D_MODEL), lambda i: (0, 0))
    return pl.pallas_call(
        functools.partial(_outproj_kernel, alpha=alpha),
        out_shape=(jax.ShapeDtypeStruct((m, D_MODEL), F32), jax.ShapeDtypeStruct((m, D_MODEL), BF16)),
        grid=(m // tm,),
        in_specs=[row, row, pl.BlockSpec((D_MODEL, D_MODEL), lambda i: (0, 0)), vec, vec],
        out_specs=(row, row),
        compiler_params=_cparams(("parallel",)),
        name="out_proj_ln1",
    )(x, merged, lp['w_out'], lp['ln1_g'], lp['ln1_b'])


def _router_kernel(x_ref, w_ref, b_ref, idx_ref, gate_ref):
    x = x_ref[...]
    logits = _dot(x, w_ref[...])
    scores = _sigmoid(logits)
    tm = x.shape[0]
    lane_i = lax.broadcasted_iota(jnp.int32, (tm, LANES), 1)
    lane = lane_i.astype(F32)
    real = lane_i < N_EXPERTS
    sel = jnp.where(real, scores + b_ref[...], NEG_INF)
    per_group = N_EXPERTS // N_GROUPS
    grp = (lane_i // per_group).astype(F32)

    def first_argmax(vals):
        m = jnp.max(vals, axis=-1, keepdims=True)
        i = jnp.min(jnp.where(vals == m, lane, float(LANES)), axis=-1, keepdims=True)
        return m, i

    gscore = jnp.full((tm, LANES), NEG_INF, F32)
    for gidx in range(N_GROUPS):
        in_g = grp == gidx
        vals = jnp.where(in_g, sel, -jnp.inf)
        m1, i1 = first_argmax(vals)
        m2 = jnp.max(jnp.where(lane == i1, -jnp.inf, vals), axis=-1, keepdims=True)
        gscore = jnp.where(in_g, m1 + m2, gscore)
    chosen = jnp.zeros((tm, LANES), jnp.bool_)
    cand = jnp.where(real, gscore, -jnp.inf)
    for _ in range(TOPK_GROUPS):
        _, i = first_argmax(cand)
        pick = grp == jnp.floor(i * (1.0 / per_group))
        chosen = jnp.logical_or(chosen, pick)
        cand = jnp.where(pick, -jnp.inf, cand)
    cand = jnp.where(real, jnp.where(chosen, sel, NEG_INF), -jnp.inf)
    idx_out = jnp.zeros((tm, LANES), F32)
    w_out = jnp.zeros((tm, LANES), F32)
    for kk in range(TOP_K):
        _, i = first_argmax(cand)
        hit = lane == i
        wk = jnp.sum(jnp.where(hit, scores, 0.0), axis=-1, keepdims=True)
        idx_out = jnp.where(lane == kk, i, idx_out)
        w_out = jnp.where(lane == kk, wk, w_out)
        cand = jnp.where(hit, -jnp.inf, cand)
    total = jnp.sum(w_out, axis=-1, keepdims=True)
    idx_ref[...] = idx_out.astype(jnp.int32)
    gate_ref[...] = w_out / total * ROUTED_SCALE


def _router_call(x, lp):
    m = x.shape[0]
    tm = _tile(m, 512)
    row = pl.BlockSpec((tm, LANES), lambda i: (i, 0))
    return pl.pallas_call(
        _router_kernel,
        out_shape=(jax.ShapeDtypeStruct((m, LANES), jnp.int32), jax.ShapeDtypeStruct((m, LANES), F32)),
        grid=(m // tm,),
        in_specs=[pl.BlockSpec((tm, D_MODEL), lambda i: (i, 0)),
                  pl.BlockSpec((D_MODEL, LANES), lambda i: (0, 0)),
                  pl.BlockSpec((1, LANES), lambda i: (0, 0))],
        out_specs=(row, row),
        compiler_params=_cparams(("parallel",)),
        name="router_topk",
    )(x, lp['router_w'], lp['router_bias'])


def _expert_kernel(be_ref, nu_ref, x_ref, wgu_ref, wdn_ref, o_ref):
    i = pl.program_id(0)

    @pl.when(i < nu_ref[0])
    def _():
        hgu = _dot(x_ref[...], wgu_ref[0])
        gt = hgu[:, :EXPERT_DIM]
        up = hgu[:, EXPERT_DIM:]
        act = (gt * _sigmoid(gt) * up).astype(BF16)
        o_ref[...] = _dot(act, wdn_ref[0]).astype(o_ref.dtype)

    @pl.when(i >= nu_ref[0])
    def _():
        o_ref[...] = jnp.zeros_like(o_ref)


def _expert_call(xs, block_expert, n_used, lp):
    n_slots = xs.shape[0]
    bm = MOE_BLOCK
    n_blocks = n_slots // bm
    grid_spec = pltpu.PrefetchScalarGridSpec(
        num_scalar_prefetch=2,
        grid=(n_blocks,),
        in_specs=[pl.BlockSpec((bm, D_MODEL), lambda i, be, nu: (i, 0)),
                  pl.BlockSpec((1, D_MODEL, 2 * EXPERT_DIM), lambda i, be, nu: (be[i], 0, 0)),
                  pl.BlockSpec((1, EXPERT_DIM, D_MODEL), lambda i, be, nu: (be[i], 0, 0))],
        out_specs=pl.BlockSpec((bm, D_MODEL), lambda i, be, nu: (i, 0)),
    )
    return pl.pallas_call(
        _expert_kernel,
        out_shape=jax.ShapeDtypeStruct((n_slots, D_MODEL), BF16),
        grid_spec=grid_spec,
        compiler_params=_cparams(("arbitrary",)),
        name="routed_experts",
    )(block_expert, n_used, xs, lp['expert_w_gate_up'], lp['expert_w_down'])


def _combine_kernel(x_ref, xb_ref, yg_ref, gate_ref, wgu_ref, wdn_ref, g_ref, b_ref, o_ref, ob_ref, *, alpha):
    x = x_ref[...]
    hgu = _dot(xb_ref[...], wgu_ref[...])
    gt = hgu[:, :SHARED_DIM]
    up = hgu[:, SHARED_DIM:]
    moe = _dot((gt * _sigmoid(gt) * up).astype(BF16), wdn_ref[...])
    gate = gate_ref[...].astype(BF16).astype(F32)
    for kk in range(TOP_K):
        moe = moe + gate[:, kk:kk + 1] * yg_ref[:, kk * D_MODEL:(kk + 1) * D_MODEL].astype(F32)
    y = _layer_norm(alpha * x + moe, g_ref[...], b_ref[...])
    o_ref[...] = y
    ob_ref[...] = y.astype(ob_ref.dtype)


def _combine_call(x, xb, yg, gate, lp, alpha):
    m = x.shape[0]
    tm = _tile(m, 128)
    row = pl.BlockSpec((tm, D_MODEL), lambda i: (i, 0))
    vec = pl.BlockSpec((1, D_MODEL), lambda i: (0, 0))
    return pl.pallas_call(
        functools.partial(_combine_kernel, alpha=alpha),
        out_shape=(jax.ShapeDtypeStruct((m, D_MODEL), F32), jax.ShapeDtypeStruct((m, D_MODEL), BF16)),
        grid=(m // tm,),
        in_specs=[row, row,
                  pl.BlockSpec((tm, TOP_K * D_MODEL), lambda i: (i, 0)),
                  pl.BlockSpec((tm, LANES), lambda i: (i, 0)),
                  pl.BlockSpec((D_MODEL, 2 * SHARED_DIM), lambda i: (0, 0)),
                  pl.BlockSpec((SHARED_DIM, D_MODEL), lambda i: (0, 0)), vec, vec],
        out_specs=(row, row),
        compiler_params=_cparams(("parallel",)),
        name="moe_combine_ln2",
    )(x, xb, yg, gate, lp['shared_w_gate_up'], lp['shared_w_down'], lp['ln2_g'], lp['ln2_b'])


def _dispatch_plan(idx):
    n_tok = idx.shape[0]
    n_a = n_tok * TOP_K
    bm = MOE_BLOCK
    n_blocks = n_a // bm + N_EXPERTS
    e_flat = idx.reshape(n_a)
    order = jnp.argsort(e_flat, stable=True).astype(jnp.int32)
    e_sorted = e_flat[order]
    counts = jnp.sum(e_flat[:, None] == jnp.arange(N_EXPERTS, dtype=jnp.int32)[None, :], axis=0,
                     dtype=jnp.int32)
    padded = (counts + bm - 1) // bm * bm
    pad_end = jnp.cumsum(padded)
    pad_start = pad_end - padded
    start = jnp.cumsum(counts) - counts
    slot = (pad_start[e_sorted] + jnp.arange(n_a, dtype=jnp.int32) - start[e_sorted]).astype(jnp.int32)
    slot_tok = jnp.zeros((n_blocks * bm,), jnp.int32).at[slot].set(order // TOP_K)
    slot_of_assign = jnp.zeros((n_a,), jnp.int32).at[order].set(slot)
    block_expert = jnp.minimum(
        jnp.searchsorted(pad_end, jnp.arange(n_blocks, dtype=jnp.int32) * bm, side='right'),
        N_EXPERTS - 1).astype(jnp.int32)
    n_used = (pad_end[-1] // bm).astype(jnp.int32).reshape(1)
    last_e = block_expert[jnp.maximum(n_used[0] - 1, 0)]
    block_expert = jnp.where(jnp.arange(n_blocks) < n_used[0], block_expert, last_e)
    return slot_tok, slot_of_assign, block_expert, n_used


def _moe_call(x1, x1b, lp, alpha):
    idx_p, gate_p = _router_call(x1b, lp)
    idx = idx_p[:, :TOP_K]
    slot_tok, slot_of_assign, block_expert, n_used = _dispatch_plan(idx)
    xs = jnp.take(x1b, slot_tok, axis=0)
    ys = _expert_call(xs, block_expert, n_used, lp)
    yg = jnp.take(ys, slot_of_assign, axis=0).reshape(x1.shape[0], TOP_K * D_MODEL)
    return _combine_call(x1, x1b, yg, gate_p, lp, alpha)


def _prep_layer(l, w_in, rwkv_mu, rwkv_w0, rwkv_w2, rwkv_a0, rwkv_a2, rwkv_g2, rwkv_k_k, rwkv_k_a,
                rwkv_r_k, rwkv_ln_g, rwkv_ln_b, ret_ln_g, ret_ln_b, att_rel_bias, w_branch_rwkv,
                w_branch_ret, w_branch_att, w_out, ln1_g, ln1_b, router_w, router_bias,
                expert_w_gate_up, expert_w_down, shared_w_gate_up, shared_w_down, ln2_g, ln2_b):
    pad_c = RWKV_PROJ_PAD - RWKV_PROJ
    wi = w_in[l]
    wi = jnp.concatenate([wi[:, :RWKV_PROJ], jnp.zeros((D_MODEL, pad_c), wi.dtype), wi[:, RWKV_PROJ:]], axis=1)
    rowv = lambda a: a.reshape(1, -1).astype(F32)
    zrows = lambda n: jnp.zeros((n, RWKV_WIDTH), F32)
    g_rows = RWKV_PROJ_PAD - 3 * RWKV_WIDTH - LANES
    return {
        'w_in': wi.astype(BF16),
        'rwkv_mu': jnp.pad(rowv(rwkv_mu[l]), ((0, 0), (0, pad_c))),
        'rwkv_w0': rowv(rwkv_w0[l]),
        'rwkv_w2': jnp.concatenate([rwkv_w2[l], zrows(RWKV_A_LORA)], 0).astype(BF16),
        'rwkv_a0': rowv(rwkv_a0[l]),
        'rwkv_a2': jnp.concatenate([zrows(RWKV_DECAY_LORA), rwkv_a2[l]], 0).astype(BF16),
        'rwkv_g2': jnp.concatenate([rwkv_g2[l], zrows(g_rows - RWKV_GATE_LORA)], 0).astype(BF16),
        'rwkv_k_k': rowv(rwkv_k_k[l]), 'rwkv_k_a': rowv(rwkv_k_a[l]), 'rwkv_r_k': rowv(rwkv_r_k[l]),
        'rwkv_ln_g': rowv(rwkv_ln_g[l]), 'rwkv_ln_b': rowv(rwkv_ln_b[l]),
        'ret_ln_g': rowv(ret_ln_g[l]), 'ret_ln_b': rowv(ret_ln_b[l]),
        'att_bias': _att_bias(att_rel_bias[l]),
        'w_branch_rwkv': w_branch_rwkv[l].astype(BF16), 'w_branch_ret': w_branch_ret[l].astype(BF16),
        'w_branch_att': w_branch_att[l].astype(BF16), 'w_out': w_out[l].astype(BF16),
        'ln1_g': rowv(ln1_g[l]), 'ln1_b': rowv(ln1_b[l]),
        'router_w': jnp.pad(router_w[l], ((0, 0), (0, LANES - N_EXPERTS))).astype(BF16),
        'router_bias': jnp.pad(rowv(router_bias[l]), ((0, 0), (0, LANES - N_EXPERTS))),
        'expert_w_gate_up': expert_w_gate_up[l].astype(BF16), 'expert_w_down': expert_w_down[l].astype(BF16),
        'shared_w_gate_up': shared_w_gate_up[l].astype(BF16), 'shared_w_down': shared_w_down[l].astype(BF16),
        'ln2_g': rowv(ln2_g[l]), 'ln2_b': rowv(ln2_b[l]),
    }


def _pad_shift(shift):
    s = jnp.pad(shift.astype(F32), ((0, 0), (0, RWKV_PROJ_PAD - RWKV_PROJ)))
    return jnp.pad(s[:, None, :], ((0, 0), (0, 7), (0, 0)))


def kernel(x_prompt, x_sample, cache_attn_k, cache_attn_v, state_rwkv, state_rwkv_shift, state_ret, w_in, rwkv_mu, rwkv_w0, rwkv_w2, rwkv_a0, rwkv_a2, rwkv_g2, rwkv_k_k, rwkv_k_a, rwkv_r_k, rwkv_ln_g, rwkv_ln_b, ret_ln_g, ret_ln_b, att_rel_bias, w_branch_rwkv, w_branch_ret, w_branch_att, w_out, ln1_g, ln1_b, router_w, router_bias, expert_w_gate_up, expert_w_down, shared_w_gate_up, shared_w_down, ln2_g, ln2_b):
    n_bp, n_s, _ = x_prompt.shape
    n_bs, n_t, _ = x_sample.shape
    depth = w_in.shape[0]
    n_p = n_bp * n_s
    n_q = n_bs * n_t
    assert n_s % ATT_QBLK == 0 and n_t == CHUNK and n_p % ATT_QBLK == 0
    alpha = float((2 * depth) ** 0.25)
    weights = (w_in, rwkv_mu, rwkv_w0, rwkv_w2, rwkv_a0, rwkv_a2, rwkv_g2, rwkv_k_k, rwkv_k_a, rwkv_r_k,
               rwkv_ln_g, rwkv_ln_b, ret_ln_g, ret_ln_b, att_rel_bias, w_branch_rwkv, w_branch_ret,
               w_branch_att, w_out, ln1_g, ln1_b, router_w, router_bias, expert_w_gate_up, expert_w_down,
               shared_w_gate_up, shared_w_down, ln2_g, ln2_b)

    x = jnp.concatenate([x_prompt.reshape(n_p, D_MODEL), x_sample.reshape(n_q, D_MODEL)], axis=0)
    xb = x.astype(BF16)
    l_c = cache_attn_k.shape[2]
    assert l_c == ATT_PAST_ROWS
    new_p = [[], [], [], [], []]
    new_s = [[], [], [], [], []]
    zeros_shift = jnp.zeros((n_bp, 8, RWKV_PROJ_PAD), F32)
    zeros_rwkv = jnp.zeros((n_bp, RWKV_HEADS // 2, RWKV_HEAD_DIM, LANES), F32)
    zeros_ret = jnp.zeros((n_bp, RET_HEADS, RET_HEAD_DIM, RET_HEAD_DIM), F32)
    for l in range(depth):
        lp = _prep_layer(l, *weights)
        h = _matmul(xb, lp['w_in'])
        hq = h[n_p:]

        ya_p, rs_p, sh_p = _rwkv_call(h, 0, n_bp, n_s, zeros_shift, zeros_rwkv, lp)
        ya_s, rs_s, sh_s = _rwkv_call(h, n_p // CHUNK, n_bs, n_t, _pad_shift(state_rwkv_shift[l]),
                                      _rwkv_state_to_pairs(state_rwkv[l]), lp)
        yr_p, ts_p = _ret_call(h, 0, n_bp, n_s, 0, zeros_ret, lp)
        yr_s, ts_s = _ret_call(h, n_p // n_t, n_bs, n_t, PAST_LEN, state_ret[l], lp)
        cq = OFF_ATT // 256
        cols = (cq, cq + ATT_WIDTH // 256, cq + 2 * ATT_WIDTH // 256)
        yc_p = _att_call(h, h, h, cols, 0, n_bp, n_s // ATT_QBLK, 0, n_s // ATT_QBLK, lp['att_bias'])
        q_s = hq[:, OFF_ATT:OFF_ATT + ATT_WIDTH].reshape(n_bs, n_t, ATT_WIDTH)
        k_s = hq[:, OFF_ATT + ATT_WIDTH:OFF_ATT + 2 * ATT_WIDTH].reshape(n_bs, n_t, ATT_WIDTH)
        v_s = hq[:, OFF_ATT + 2 * ATT_WIDTH:OFF_ATT + 3 * ATT_WIDTH].reshape(n_bs, n_t, ATT_WIDTH)
        win_rows = 3 * ATT_QBLK
        lead = ATT_PAST_ROWS - l_c
        tail = win_rows - ATT_PAST_ROWS - n_t
        padrows = lambda a, lo, hi: jnp.pad(a, ((0, 0), (lo, hi), (0, 0))).reshape(n_bs * win_rows, ATT_WIDTH)
        q_w = padrows(q_s, ATT_PAST_ROWS, tail)
        k_w = padrows(jnp.concatenate([cache_attn_k[l].reshape(n_bs, l_c, ATT_WIDTH).astype(F32), k_s], 1), lead, tail)
        v_w = padrows(jnp.concatenate([cache_attn_v[l].reshape(n_bs, l_c, ATT_WIDTH).astype(F32), v_s], 1), lead, tail)
        yc_s = _att_call(q_w, k_w, v_w, (0, 0, 0), 0, n_bs, 3, 2, 1, lp['att_bias'])
        yc_s = yc_s.reshape(n_bs, ATT_QBLK, ATT_WIDTH)[:, :n_t].reshape(n_q, ATT_WIDTH)

        ya = jnp.concatenate([ya_p, ya_s], axis=0)
        yr = jnp.concatenate([yr_p, yr_s], axis=0)
        yc = jnp.concatenate([yc_p, yc_s], axis=0)
        merged = _merge_call(ya, yr, yc, h, lp)
        x1, x1b = _outproj_call(x, merged, lp, alpha)
        x, xb = _moe_call(x1, x1b, lp, alpha)

        hp3 = h[:n_p].reshape(n_bp, n_s, IN_PROJ_PAD)
        keep = min(ATT_PAST_ROWS, n_s)
        kp = hp3[:, n_s - keep:, OFF_ATT + ATT_WIDTH:OFF_ATT + 2 * ATT_WIDTH]
        vp = hp3[:, n_s - keep:, OFF_ATT + 2 * ATT_WIDTH:OFF_ATT + 3 * ATT_WIDTH]
        st_p = (kp.reshape(n_bp, keep, ATT_HEADS, ATT_HEAD_DIM), vp.reshape(n_bp, keep, ATT_HEADS, ATT_HEAD_DIM),
                _rwkv_state_from_pairs(rs_p), sh_p[:, 0, :RWKV_PROJ], ts_p)
        st_s = (k_s.reshape(n_bs, n_t, ATT_HEADS, ATT_HEAD_DIM), v_s.reshape(n_bs, n_t, ATT_HEADS, ATT_HEAD_DIM),
                _rwkv_state_from_pairs(rs_s), sh_s[:, 0, :RWKV_PROJ], ts_s)
        for lst, arr in zip(new_p, st_p):
            lst.append(arr)
        for lst, arr in zip(new_s, st_s):
            lst.append(arr)
    yp = x[:n_p].reshape(n_bp, n_s, D_MODEL)
    ys = x[n_p:].reshape(n_bs, n_t, D_MODEL)
    outs_p = [jnp.stack(a, 0) for a in new_p]
    outs_s = [jnp.stack(a, 0) for a in new_s]
    return (yp, ys, *outs_p, *outs_s)
```

```python
import functools
import math

import jax
import jax.numpy as jnp
import numpy as np
from jax import lax
from jax.experimental import pallas as pl
from jax.experimental.pallas import tpu as pltpu

F32 = jnp.float32
BF16 = jnp.bfloat16

D_MODEL = 2048
PAST_LEN = 1024
CHUNK = 64
RWKV_WIDTH = 1024
RWKV_HEAD_DIM = 64
RWKV_HEADS = 16
RWKV_DECAY_LORA = 64
RWKV_A_LORA = 64
RWKV_GATE_LORA = 160
RWKV_PROJ = 3 * RWKV_WIDTH + RWKV_DECAY_LORA + RWKV_A_LORA + RWKV_GATE_LORA
RWKV_PROJ_PAD = 3584
RWKV_GN_EPS = 64e-5
RET_HEAD_DIM = 128
RET_WIDTH = 1024
RET_HEADS = 8
RET_GN_EPS = 1e-5
ROPE_BASE = 10000.0
RET_CHUNK = 64
ATT_HEAD_DIM = 64
ATT_WIDTH = 1024
ATT_HEADS = 16
ATT_LEFT_CHUNKS = 8
ATT_PAST_ROWS = ATT_LEFT_CHUNKS * CHUNK
ATT_QBLK = 256
ATT_WIN = ATT_QBLK + ATT_PAST_ROWS
REL_CLIP = 128
OFF_RET = RWKV_PROJ_PAD
OFF_ATT = OFF_RET + 4 * RET_WIDTH
OFF_GATE = OFF_ATT + 3 * ATT_WIDTH
IN_PROJ_PAD = OFF_GATE + 3 * D_MODEL
N_EXPERTS = 64
TOP_K = 8
N_GROUPS = 8
TOPK_GROUPS = 4
EXPERT_DIM = 512
SHARED_DIM = 512
ROUTED_SCALE = 2.5
MOE_BLOCK = 512
LN_EPS = 1e-5
NEG_INF = -1e30
LANES = 128

VMEM_LIMIT = 56 * 1024 * 1024


def _cparams(sem):
    return pltpu.CompilerParams(dimension_semantics=sem, vmem_limit_bytes=VMEM_LIMIT)


def _tile(n, target, align=8):
    for t in range(min(n, target), 0, -1):
        if n % t == 0 and t % align == 0:
            return t
    return n


def _dot(a, b):
    return jnp.dot(a, b, preferred_element_type=F32)


def _dot_nt(a, b):
    return lax.dot_general(a, b, (((1,), (1,)), ((), ())), preferred_element_type=F32)


def _dot_tn(a, b):
    return lax.dot_general(a, b, (((0,), (0,)), ((), ())), preferred_element_type=F32)


def _sigmoid(x):
    return 1.0 / (1.0 + jnp.exp(-x))


def _split3(x):
    hi = x.astype(BF16)
    r1 = x - hi.astype(F32)
    mid = r1.astype(BF16)
    lo = (r1 - mid.astype(F32)).astype(BF16)
    return hi, mid, lo


def _mm_kernel(x_ref, w_ref, o_ref):
    o_ref[...] = _dot(x_ref[...], w_ref[...])


def _matmul(x, w, tm_target=1280, tn_target=768):
    m, k = x.shape
    n = w.shape[1]
    tm = _tile(m, tm_target)
    tn = _tile(n, tn_target, LANES)
    return pl.pallas_call(
        _mm_kernel,
        out_shape=jax.ShapeDtypeStruct((m, n), F32),
        grid=(m // tm, n // tn),
        in_specs=[pl.BlockSpec((tm, k), lambda i, j: (i, 0)),
                  pl.BlockSpec((k, tn), lambda i, j: (0, j))],
        out_specs=pl.BlockSpec((tm, tn), lambda i, j: (i, j)),
        compiler_params=_cparams(("parallel", "parallel")),
        name="in_proj",
    )(x, w)


def _rwkv_kernel(*refs, n_rows):
    p_refs = refs[:n_rows]
    (shift_ref, s0_ref, mu_ref, w0_ref, w2_ref, a0_ref, a2_ref, g2_ref, kk_ref, ka_ref, rk_ref,
     lng_ref, lnb_ref, e_ref, et_ref, ones2_ref, ones3_ref,
     y_ref, sout_ref, shout_ref,
     s_scr, sb_scr, prev_scr, ar_scr, r_scr, w_scr, b_scr, k_scr, v1_scr, v2_scr, v3_scr,
     yo_scr, g_scr) = refs[n_rows:]
    c = pl.program_id(1)
    n_c = pl.num_programs(1)
    C = p_refs[0].shape[0]
    n_pairs = RWKV_WIDTH // LANES
    hd = RWKV_HEAD_DIM

    @pl.when(c == 0)
    def _():
        s_scr[...] = s0_ref[...]
        sb_scr[...] = s0_ref[...].astype(BF16)
        prev_scr[...] = shift_ref[...]

    e_m = e_ref[...]
    et_m = et_ref[...]

    def headsum(x):
        s = None
        for limb in _split3(x):
            t = _dot(limb, e_m)
            s = t if s is None else s + t
        out = None
        for limb in _split3(s):
            t = _dot(limb, et_m)
            out = t if out is None else out + t
        return out

    w = RWKV_WIDTH
    for rr in range(n_rows):
        pf = p_refs[rr][...]
        row = lax.broadcasted_iota(jnp.int32, pf.shape, 0)
        prev = jnp.where(row == 0, prev_scr[rr, 0:1, :], pltpu.roll(pf, 1, 0))
        prev_scr[rr, 0:1, :] = pf[C - 1:C, :]
        px = pf + (prev - pf) * mu_ref[...]
        r = px[:, 0:w]
        k = px[:, w:2 * w]
        v = px[:, 2 * w:3 * w]
        lora = px[:, 3 * w:3 * w + LANES]
        xg = px[:, 3 * w + LANES:RWKV_PROJ_PAD]
        z = w0_ref[...] + _dot(jnp.tanh(lora).astype(BF16), w2_ref[...])
        w_log = -(jnp.maximum(-z, 0.0) + jnp.log1p(jnp.exp(-jnp.abs(z)))) - 0.5
        a = _sigmoid(a0_ref[...] + _dot(lora.astype(BF16), a2_ref[...]))
        g_scr[rr] = _dot(_sigmoid(xg).astype(BF16), g2_ref[...])
        kk = k * kk_ref[...]
        kk = kk / jnp.maximum(jnp.sqrt(headsum(kk * kk)), 1e-12)
        ar_scr[rr] = (-kk).astype(BF16).astype(F32)
        r_scr[rr] = r
        w_scr[rr] = jnp.exp(-jnp.exp(w_log))
        b_scr[rr] = kk * a
        k_scr[rr] = k * (1.0 + (a - 1.0) * ka_ref[...])
        v1, v2, v3 = _split3(v)
        v1_scr[rr] = v1.astype(F32)
        v2_scr[rr] = v2.astype(F32)
        v3_scr[rr] = v3.astype(F32)

    lane = lax.broadcasted_iota(jnp.int32, (hd, LANES), 1)
    sub = lax.broadcasted_iota(jnp.int32, (hd, LANES), 0)
    mdiag = (lane & (hd - 1)) == sub
    lane8 = lax.broadcasted_iota(jnp.int32, (8, LANES), 1)
    sub8 = lax.broadcasted_iota(jnp.int32, (8, LANES), 0)
    hsel = (lane8 // hd) == sub8
    ones2 = ones2_ref[...]
    ones3 = ones3_ref[...]
    pairs = [(rr, p) for rr in range(n_rows) for p in range(n_pairs)]
    yo_scr[...] = jnp.zeros(yo_scr.shape, F32)

    def step(t, carry):
        t0 = pl.multiple_of((t // 8) * 8, 8)
        tj = t - t0
        to_top = (8 - tj) & 7

        def row(scr, rr, ls):
            return pltpu.roll(scr[rr, pl.ds(t0, 8), ls], to_top, 0)[0:1, :]

        hl, vd = [], []
        for rr, p in pairs:
            ls = slice(p * LANES, (p + 1) * LANES)
            prod = sb_scr[rr, p].astype(F32) * row(ar_scr, rr, ls)
            bits = lax.bitcast_convert_type(prod, jnp.uint32) & jnp.uint32(0xFFFF0000)
            hi = lax.bitcast_convert_type(bits, F32)
            hl.append(jnp.concatenate([hi.astype(BF16), (prod - hi).astype(BF16)], axis=1))
            vd.append(jnp.concatenate([jnp.where(mdiag, row(v1_scr, rr, ls), 0.0),
                                       jnp.where(mdiag, row(v2_scr, rr, ls), 0.0),
                                       jnp.where(mdiag, row(v3_scr, rr, ls), 0.0)],
                                      axis=1).astype(BF16))
        sa_all = _dot(jnp.concatenate(hl, axis=0), ones2)
        vb_all = _dot(jnp.concatenate(vd, axis=0), ones3)
        for i, (rr, p) in enumerate(pairs):
            ls = slice(p * LANES, (p + 1) * LANES)
            rows = slice(i * hd, (i + 1) * hd)
            s_new = ((s_scr[rr, p] * row(w_scr, rr, ls) + sa_all[rows] * row(b_scr, rr, ls))
                     + vb_all[rows] * row(k_scr, rr, ls))
            s_scr[rr, p] = s_new
            sb = s_new.astype(BF16)
            sb_scr[rr, p] = sb
            r_rows = jnp.where(hsel, row(r_scr, rr, ls), 0.0).astype(BF16)
            y2 = _dot_nt(r_rows, sb)
            y_row = jnp.concatenate([y2[0:1, :], y2[1:2, :]], axis=1)
            yo_scr[rr, pl.ds(t0, 8), ls] = jnp.where(sub8 == tj, y_row, yo_scr[rr, pl.ds(t0, 8), ls])
        return carry

    lax.fori_loop(0, C, step, 0)

    inv_n = 1.0 / hd
    for rr in range(n_rows):
        y = yo_scr[rr]
        mean = headsum(y) * inv_n
        d = y - mean
        var = headsum(d * d) * inv_n
        yn = d * lax.rsqrt(var + RWKV_GN_EPS) * lng_ref[...] + lnb_ref[...]
        v = (v1_scr[rr] + v2_scr[rr]) + v3_scr[rr]
        bonus = headsum(r_scr[rr] * k_scr[rr] * rk_ref[...]) * v
        y_ref[rr] = ((yn + bonus) * g_scr[rr]).astype(y_ref.dtype)

    @pl.when(c == n_c - 1)
    def _():
        sout_ref[...] = s_scr[...]
        shout_ref[...] = prev_scr[...]


def _rwkv_consts():
    lane = np.arange(RWKV_WIDTH)
    e = (lane[:, None] // RWKV_HEAD_DIM == np.arange(LANES)[None, :]).astype(np.float32)
    gl = np.arange(LANES) // RWKV_HEAD_DIM
    ones = (gl[:, None] == gl[None, :]).astype(np.float32)
    return (jnp.asarray(e, BF16), jnp.asarray(e.T, BF16),
            jnp.asarray(np.concatenate([ones] * 2, 0), BF16), jnp.asarray(np.concatenate([ones] * 3, 0), BF16))


def _rwkv_call(h, row_blk0, n_b, n_t, shift0, s0_pairs, lp):
    C = CHUNK
    n_c = n_t // C
    n_rows = 2 if n_b % 2 == 0 else 1
    n_pairs = RWKV_WIDTH // LANES
    consts = _rwkv_consts()
    full = lambda arr: pl.BlockSpec(arr.shape, lambda i, c: (0,) * arr.ndim)
    params = [lp['rwkv_mu'], lp['rwkv_w0'], lp['rwkv_w2'], lp['rwkv_a0'], lp['rwkv_a2'], lp['rwkv_g2'],
              lp['rwkv_k_k'], lp['rwkv_k_a'], lp['rwkv_r_k'], lp['rwkv_ln_g'], lp['rwkv_ln_b']]

    def pspec(rr):
        return pl.BlockSpec((C, RWKV_PROJ_PAD), lambda i, c: (row_blk0 + (i * n_rows + rr) * n_c + c, 0))
    state_spec = pl.BlockSpec((n_rows, n_pairs, RWKV_HEAD_DIM, LANES), lambda i, c: (i, 0, 0, 0))
    shift_spec = pl.BlockSpec((n_rows, 8, RWKV_PROJ_PAD), lambda i, c: (i, 0, 0))
    rows_scr = pltpu.VMEM((n_rows, C, RWKV_WIDTH), F32)
    y, s_out, sh_out = pl.pallas_call(
        functools.partial(_rwkv_kernel, n_rows=n_rows),
        out_shape=(jax.ShapeDtypeStruct((n_b, n_t, RWKV_WIDTH), BF16),
                   jax.ShapeDtypeStruct((n_b, n_pairs, RWKV_HEAD_DIM, LANES), F32),
                   jax.ShapeDtypeStruct((n_b, 8, RWKV_PROJ_PAD), F32)),
        grid=(n_b // n_rows, n_c),
        in_specs=[pspec(rr) for rr in range(n_rows)] + [shift_spec, state_spec]
                 + [full(a) for a in params] + [full(a) for a in consts],
        out_specs=(pl.BlockSpec((n_rows, C, RWKV_WIDTH), lambda i, c: (i, c, 0)), state_spec, shift_spec),
        scratch_shapes=[pltpu.VMEM((n_rows, n_pairs, RWKV_HEAD_DIM, LANES), F32),
                        pltpu.VMEM((n_rows, n_pairs, RWKV_HEAD_DIM, LANES), BF16),
                        pltpu.VMEM((n_rows, 8, RWKV_PROJ_PAD), F32)] + [rows_scr] * 10,
        compiler_params=_cparams(("parallel", "arbitrary")),
        name="rwkv7_scan",
    )(*([h] * n_rows), shift0, s0_pairs, *params, *consts)
    return y.reshape(n_b * n_t, RWKV_WIDTH), s_out, sh_out


def _rwkv_state_to_pairs(s):
    n_b = s.shape[0]
    s5 = s.astype(F32).reshape(n_b, RWKV_HEADS // 2, 2, RWKV_HEAD_DIM, RWKV_HEAD_DIM)
    return s5.transpose(0, 1, 3, 2, 4).reshape(n_b, RWKV_HEADS // 2, RWKV_HEAD_DIM, LANES)


def _rwkv_state_from_pairs(sp):
    n_b = sp.shape[0]
    s5 = sp.reshape(n_b, RWKV_HEADS // 2, RWKV_HEAD_DIM, 2, RWKV_HEAD_DIM)
    return s5.transpose(0, 1, 3, 2, 4).reshape(n_b, RWKV_HEADS, RWKV_HEAD_DIM, RWKV_HEAD_DIM)


def _ret_kernel(q_ref, k_ref, v_ref, g_ref, cc_ref, ss_ref, dec_ref, qd_ref, kd_ref, s0_ref,
                lng_ref, lnb_ref, y_ref, sout_ref, s_scr, *, chunk_decay):
    hh = pl.program_id(1)
    c = pl.program_id(2)
    n_c = pl.num_programs(2)
    n_h = s_scr.shape[0]
    d = RET_HEAD_DIM

    @pl.when(c == 0)
    def _():
        s_scr[...] = s0_ref[0]

    cc = cc_ref[...]
    ss = ss_ref[...]
    outs = []
    for j in range(n_h):
        sl = slice(j * d, (j + 1) * d)
        q = q_ref[:, sl]
        k = k_ref[:, sl]
        v = v_ref[:, sl].astype(BF16)
        qr = q * cc + pltpu.roll(q, d // 2, 1) * ss
        kr = (k * cc + pltpu.roll(k, d // 2, 1) * ss) * (d ** -0.5)
        qb = qr.astype(BF16)
        scores = _dot_nt(qb, kr.astype(BF16)) * dec_ref[j]
        inner = _dot(scores.astype(BF16), v)
        s_j = s_scr[j]
        cross = _dot((qr * qd_ref[j]).astype(BF16), s_j.astype(BF16))
        kv = _dot_tn((kr * kd_ref[j]).astype(BF16), v)
        cd = jnp.where(hh == 0, chunk_decay[j], chunk_decay[n_h + j])
        s_scr[j] = s_j * cd + kv
        y = inner + cross
        mu = jnp.mean(y, axis=-1, keepdims=True)
        yc = y - mu
        var = jnp.mean(yc * yc, axis=-1, keepdims=True)
        outs.append(yc * lax.rsqrt(var + RET_GN_EPS))
    yn = jnp.concatenate(outs, axis=1) * lng_ref[...] + lnb_ref[...]
    gt = g_ref[...]
    y_ref[...] = (gt * _sigmoid(gt) * yn).astype(y_ref.dtype)

    @pl.when(c == n_c - 1)
    def _():
        sout_ref[0] = s_scr[...]


def _ret_consts(C, pos0, n_t):
    f = np.float32
    log_gamma = np.log1p(-np.exp2(-5.0 - np.arange(RET_HEADS, dtype=f))).astype(f)
    idx = np.arange(C, dtype=f)
    diff = idx[:, None] - idx[None, :]
    dec = np.where(diff >= 0, np.exp(log_gamma[:, None, None] * np.maximum(diff, 0.0)), 0.0).astype(f)
    qd = np.exp(log_gamma[:, None] * (idx + 1.0)[None, :]).astype(f)
    kd = np.exp(log_gamma[:, None] * (C - 1.0 - idx)[None, :]).astype(f)
    qd = np.broadcast_to(qd[:, :, None], (RET_HEADS, C, RET_HEAD_DIM)).copy()
    kd = np.broadcast_to(kd[:, :, None], (RET_HEADS, C, RET_HEAD_DIM)).copy()
    cd = tuple(float(x) for x in np.exp(log_gamma * C))
    half = RET_HEAD_DIM // 2
    inv = (ROPE_BASE ** (-np.arange(half, dtype=f) / half)).astype(f)
    pos = (pos0 + np.arange(n_t)).astype(f)
    ang = pos[:, None] * inv[None, :]
    cos, sin = np.cos(ang).astype(f), np.sin(ang).astype(f)
    cc = np.concatenate([cos, cos], axis=1)
    ss = np.concatenate([-sin, sin], axis=1)
    return jnp.asarray(dec), jnp.asarray(qd), jnp.asarray(kd), cd, jnp.asarray(cc), jnp.asarray(ss)


def _ret_call(h, row_blk0, n_b, n_t, pos0, s0, lp):
    C = min(RET_CHUNK, n_t)
    n_c = n_t // C
    hw = 512
    n_hh = RET_WIDTH // hw
    hp = hw // RET_HEAD_DIM
    dec, qd, kd, cd, cc, ss = _ret_consts(C, pos0, n_t)
    col0 = OFF_RET // hw
    blk = lambda part: pl.BlockSpec(
        (C, hw), lambda b, hh, c: (row_blk0 + b * n_c + c, col0 + part * n_hh + hh))
    y, s_out = pl.pallas_call(
        functools.partial(_ret_kernel, chunk_decay=cd),
        out_shape=(jax.ShapeDtypeStruct((n_b * n_t, RET_WIDTH), BF16),
                   jax.ShapeDtypeStruct((n_b, RET_HEADS, RET_HEAD_DIM, RET_HEAD_DIM), F32)),
        grid=(n_b, n_hh, n_c),
        in_specs=[blk(0), blk(1), blk(2), blk(3),
                  pl.BlockSpec((C, RET_HEAD_DIM), lambda b, hh, c: (c, 0)),
                  pl.BlockSpec((C, RET_HEAD_DIM), lambda b, hh, c: (c, 0)),
                  pl.BlockSpec((hp, C, C), lambda b, hh, c: (hh, 0, 0)),
                  pl.BlockSpec((hp, C, RET_HEAD_DIM), lambda b, hh, c: (hh, 0, 0)),
                  pl.BlockSpec((hp, C, RET_HEAD_DIM), lambda b, hh, c: (hh, 0, 0)),
                  pl.BlockSpec((1, hp, RET_HEAD_DIM, RET_HEAD_DIM), lambda b, hh, c: (b, hh, 0, 0)),
                  pl.BlockSpec((1, hw), lambda b, hh, c: (0, hh)),
                  pl.BlockSpec((1, hw), lambda b, hh, c: (0, hh))],
        out_specs=(pl.BlockSpec((C, hw), lambda b, hh, c: (b * n_c + c, hh)),
                   pl.BlockSpec((1, hp, RET_HEAD_DIM, RET_HEAD_DIM), lambda b, hh, c: (b, hh, 0, 0))),
        scratch_shapes=[pltpu.VMEM((hp, RET_HEAD_DIM, RET_HEAD_DIM), F32)],
        compiler_params=_cparams(("parallel", "parallel", "arbitrary")),
        name="retention_chunk",
    )(h, h, h, h, cc, ss, dec, qd, kd, s0.astype(F32), lp['ret_ln_g'], lp['ret_ln_b'])
    return y, s_out


def _att_kernel(q_ref, k0_ref, k1_ref, k2_ref, v0_ref, v1_ref, v2_ref, bias_ref, o_ref, *, blk0):
    blk = pl.program_id(2) + blk0
    hd = ATT_HEAD_DIM
    n_pairs = q_ref.shape[1] // LANES
    lane = lax.broadcasted_iota(jnp.int32, (1, LANES), 1)
    k_refs = (k0_ref, k1_ref, k2_ref)
    v_refs = (v0_ref, v1_ref, v2_ref)
    valid = (blk >= 2, blk >= 1, None)
    outs = []
    for p in range(n_pairs):
        sl = slice(p * LANES, (p + 1) * LANES)
        q = q_ref[:, sl]
        ks = [kr[:, sl].astype(BF16) for kr in k_refs]
        vs = [vr[:, sl].astype(BF16) for vr in v_refs]
        o_heads = []
        for j in range(2):
            qm = jnp.where((lane // hd) == j, q, 0.0).astype(BF16)
            parts = []
            for kb in range(3):
                s = _dot_nt(qm, ks[kb]) * (hd ** -0.5)
                s = s + bias_ref[2 * p + j, :, kb * ATT_QBLK:(kb + 1) * ATT_QBLK]
                if valid[kb] is not None:
                    s = jnp.where(valid[kb], s, NEG_INF)
                parts.append(s)
            s = jnp.concatenate(parts, axis=1)
            m = jnp.max(s, axis=-1, keepdims=True)
            e = jnp.exp(s - m)
            pr = (e / jnp.sum(e, axis=-1, keepdims=True)).astype(BF16)
            o = _dot(pr[:, 0:ATT_QBLK], vs[0])
            o = o + _dot(pr[:, ATT_QBLK:2 * ATT_QBLK], vs[1])
            o = o + _dot(pr[:, 2 * ATT_QBLK:3 * ATT_QBLK], vs[2])
            o_heads.append(o)
        outs.append(jnp.where((lane // hd) == 0, o_heads[0], o_heads[1]))
    o_ref[...] = jnp.concatenate(outs, axis=1).astype(o_ref.dtype)


def _att_bias(table):
    n_h = table.shape[0]
    r = np.arange(ATT_QBLK)[:, None]
    w = np.arange(ATT_WIN)[None, :]
    lo = CHUNK * (r // CHUNK)
    band = (w >= lo) & (w < lo + ATT_PAST_ROWS + CHUNK)
    period = ATT_QBLK + ATT_WIN + 1
    j = np.arange(period)
    d_idx = np.clip(ATT_QBLK - j + ATT_PAST_ROWS, -REL_CLIP, REL_CLIP) + REL_CLIP
    d = table.astype(F32)[:, d_idx]
    skew = jnp.tile(d, (1, ATT_QBLK))[:, :ATT_QBLK * (period - 1)].reshape(n_h, ATT_QBLK, period - 1)
    b = skew[:, :, ATT_QBLK:ATT_QBLK + ATT_WIN]
    return jnp.where(jnp.asarray(band)[None], b, NEG_INF)


def _att_call(q_arr, k_arr, v_arr, cols, row_blk0, n_b, n_blk_batch, blk0, n_blk, bias):
    gw = 256
    n_hg = ATT_WIDTH // gw
    hpg = gw // ATT_HEAD_DIM
    qc, kc, vc = cols
    rowb = lambda b, i: row_blk0 + b * n_blk_batch + i
    qspec = pl.BlockSpec((ATT_QBLK, gw), lambda g, b, i: (rowb(b, i + blk0), qc + g))

    def kvspec(col, back):
        return pl.BlockSpec((ATT_QBLK, gw),
                            lambda g, b, i: (rowb(b, jnp.maximum(i + blk0 - back, 0)), col + g))
    return pl.pallas_call(
        functools.partial(_att_kernel, blk0=blk0),
        out_shape=jax.ShapeDtypeStruct((n_b * n_blk * ATT_QBLK, ATT_WIDTH), BF16),
        grid=(n_hg, n_b, n_blk),
        in_specs=[qspec, kvspec(kc, 2), kvspec(kc, 1), kvspec(kc, 0),
                  kvspec(vc, 2), kvspec(vc, 1), kvspec(vc, 0),
                  pl.BlockSpec((hpg, ATT_QBLK, ATT_WIN), lambda g, b, i: (g, 0, 0))],
        out_specs=pl.BlockSpec((ATT_QBLK, gw), lambda g, b, i: (b * n_blk + i, g)),
        compiler_params=_cparams(("parallel", "parallel", "arbitrary")),
        name="band_attention",
    )(q_arr, k_arr, k_arr, k_arr, v_arr, v_arr, v_arr, bias)


def _merge_kernel(ya_ref, yr_ref, yc_ref, ga_ref, gr_ref, gc_ref, wa_ref, wr_ref, wc_ref, o_ref):
    m = _sigmoid(ga_ref[...]) * _dot(ya_ref[...], wa_ref[...])
    m = m + _sigmoid(gr_ref[...]) * _dot(yr_ref[...], wr_ref[...])
    m = m + _sigmoid(gc_ref[...]) * _dot(yc_ref[...], wc_ref[...])
    o_ref[...] = m.astype(o_ref.dtype)


def _merge_call(ya, yr, yc, h, lp):
    m = ya.shape[0]
    tm = _tile(m, 512)
    tn = 512
    n_n = D_MODEL // tn
    g0 = OFF_GATE // tn
    yspec = pl.BlockSpec((tm, RWKV_WIDTH), lambda i, j: (i, 0))
    wspec = pl.BlockSpec((RWKV_WIDTH, tn), lambda i, j: (0, j))
    gspec = lambda part: pl.BlockSpec((tm, tn), lambda i, j: (i, g0 + part * n_n + j))
    return pl.pallas_call(
        _merge_kernel,
        out_shape=jax.ShapeDtypeStruct((m, D_MODEL), BF16),
        grid=(m // tm, n_n),
        in_specs=[yspec, yspec, yspec, gspec(0), gspec(1), gspec(2), wspec, wspec, wspec],
        out_specs=pl.BlockSpec((tm, tn), lambda i, j: (i, j)),
        compiler_params=_cparams(("parallel", "parallel")),
        name="branch_merge",
    )(ya, yr, yc, h, h, h, lp['w_branch_rwkv'], lp['w_branch_ret'], lp['w_branch_att'])


def _layer_norm(z, g, b):
    mu = jnp.mean(z, axis=-1, keepdims=True)
    zc = z - mu
    var = jnp.mean(zc * zc, axis=-1, keepdims=True)
    return zc * lax.rsqrt(var + LN_EPS) * g + b


def _outproj_kernel(x_ref, m_ref, w_ref, g_ref, b_ref, o_ref, ob_ref, *, alpha):
    z = alpha * x_ref[...] + _dot(m_ref[...], w_ref[...])
    y = _layer_norm(z, g_ref[...], b_ref[...])
    o_ref[...] = y
    ob_ref[...] = y.astype(ob_ref.dtype)


def _outproj_call(x, merged, lp, alpha):
    m = x.shape[0]
    tm = _tile(m, 256)
    row = pl.BlockSpec((tm, D_MODEL), lambda i: (i, 0))
    vec = pl.BlockSpec((1, D_MODEL), lambda i: (0, 0))
    return pl.pallas_call(
        functools.partial(_outproj_kernel, alpha=alpha),
        out_shape=(jax.ShapeDtypeStruct((m, D_MODEL), F32), jax.ShapeDtypeStruct((m, D_MODEL), BF16)),
        grid=(m // tm,),
        in_specs=[row, row, pl.BlockSpec((D_MODEL, D_MODEL), lambda i: (0, 0)), vec, vec],
        out_specs=(row, row),
        compiler_params=_cparams(("parallel",)),
        name="out_proj_ln1",
    )(x, merged, lp['w_out'], lp['ln1_g'], lp['ln1_b'])


def _router_kernel(x_ref, w_ref, b_ref, idx_ref, gate_ref):
    x = x_ref[...]
    logits = _dot(x, w_ref[...])
    scores = _sigmoid(logits)
    tm = x.shape[0]
    lane_i = lax.broadcasted_iota(jnp.int32, (tm, LANES), 1)
    lane = lane_i.astype(F32)
    real = lane_i < N_EXPERTS
    sel = jnp.where(real, scores + b_ref[...], NEG_INF)
    per_group = N_EXPERTS // N_GROUPS
    grp = (lane_i // per_group).astype(F32)

    def first_argmax(vals):
        m = jnp.max(vals, axis=-1, keepdims=True)
        i = jnp.min(jnp.where(vals == m, lane, float(LANES)), axis=-1, keepdims=True)
        return m, i

    gscore = jnp.full((tm, LANES), NEG_INF, F32)
    for gidx in range(N_GROUPS):
        in_g = grp == gidx
        vals = jnp.where(in_g, sel, -jnp.inf)
        m1, i1 = first_argmax(vals)
        m2 = jnp.max(jnp.where(lane == i1, -jnp.inf, vals), axis=-1, keepdims=True)
        gscore = jnp.where(in_g, m1 + m2, gscore)
    chosen = jnp.zeros((tm, LANES), jnp.bool_)
    cand = jnp.where(real, gscore, -jnp.inf)
    for _ in range(TOPK_GROUPS):
        _, i = first_argmax(cand)
        pick = grp == jnp.floor(i * (1.0 / per_group))
        chosen = jnp.logical_or(chosen, pick)
        cand = jnp.where(pick, -jnp.inf, cand)
    cand = jnp.where(real, jnp.where(chosen, sel, NEG_INF), -jnp.inf)
    idx_out = jnp.zeros((tm, LANES), F32)
    w_out = jnp.zeros((tm, LANES), F32)
    for kk in range(TOP_K):
        _, i = first_argmax(cand)
        hit = lane == i
        wk = jnp.sum(jnp.where(hit, scores, 0.0), axis=-1, keepdims=True)
        idx_out = jnp.where(lane == kk, i, idx_out)
        w_out = jnp.where(lane == kk, wk, w_out)
        cand = jnp.where(hit, -jnp.inf, cand)
    total = jnp.sum(w_out, axis=-1, keepdims=True)
    idx_ref[...] = idx_out.astype(jnp.int32)
    gate_ref[...] = w_out / total * ROUTED_SCALE


def _router_call(x, lp):
    m = x.shape[0]
    tm = _tile(m, 512)
    row = pl.BlockSpec((tm, LANES), lambda i: (i, 0))
    return pl.pallas_call(
        _router_kernel,
        out_shape=(jax.ShapeDtypeStruct((m, LANES), jnp.int32), jax.ShapeDtypeStruct((m, LANES), F32)),
        grid=(m // tm,),
        in_specs=[pl.BlockSpec((tm, D_MODEL), lambda i: (i, 0)),
                  pl.BlockSpec((D_MODEL, LANES), lambda i: (0, 0)),
                  pl.BlockSpec((1, LANES), lambda i: (0, 0))],
        out_specs=(row, row),
        compiler_params=_cparams(("parallel",)),
        name="router_topk",
    )(x, lp['router_w'], lp['router_bias'])


def _expert_kernel(be_ref, nu_ref, x_ref, wgu_ref, wdn_ref, o_ref, wgu_scr, wdn_scr):
    i = pl.program_id(0)

    @pl.when(jnp.logical_or(i == 0, be_ref[i] != be_ref[jnp.maximum(i - 1, 0)]))
    def _():
        wgu_scr[...] = wgu_ref[0, 0].astype(BF16)
        wdn_scr[...] = wdn_ref[0, 0].astype(BF16)

    @pl.when(i < nu_ref[0])
    def _():
        hgu = _dot(x_ref[...], wgu_scr[...])
        gt = hgu[:, :EXPERT_DIM]
        up = hgu[:, EXPERT_DIM:]
        act = (gt * _sigmoid(gt) * up).astype(BF16)
        o_ref[...] = _dot(act, wdn_scr[...]).astype(o_ref.dtype)

    @pl.when(i >= nu_ref[0])
    def _():
        o_ref[...] = jnp.zeros_like(o_ref)


def _expert_call(xs, block_expert, n_used, lp):
    n_slots = xs.shape[0]
    bm = MOE_BLOCK
    n_blocks = n_slots // bm
    layer = lp['layer']
    grid_spec = pltpu.PrefetchScalarGridSpec(
        num_scalar_prefetch=2,
        grid=(n_blocks,),
        in_specs=[pl.BlockSpec((bm, D_MODEL), lambda i, be, nu: (i, 0)),
                  pl.BlockSpec((1, 1, D_MODEL, 2 * EXPERT_DIM), lambda i, be, nu: (layer, be[i], 0, 0)),
                  pl.BlockSpec((1, 1, EXPERT_DIM, D_MODEL), lambda i, be, nu: (layer, be[i], 0, 0))],
        out_specs=pl.BlockSpec((bm, D_MODEL), lambda i, be, nu: (i, 0)),
        scratch_shapes=[pltpu.VMEM((D_MODEL, 2 * EXPERT_DIM), BF16), pltpu.VMEM((EXPERT_DIM, D_MODEL), BF16)],
    )
    return pl.pallas_call(
        _expert_kernel,
        out_shape=jax.ShapeDtypeStruct((n_slots, D_MODEL), BF16),
        grid_spec=grid_spec,
        compiler_params=_cparams(("arbitrary",)),
        name="routed_experts",
    )(block_expert, n_used, xs, lp['expert_w_gate_up'], lp['expert_w_down'])


def _combine_kernel(x_ref, xb_ref, yg_ref, gate_ref, wgu_ref, wdn_ref, g_ref, b_ref, o_ref, ob_ref, *, alpha):
    x = x_ref[...]
    hgu = _dot(xb_ref[...], wgu_ref[...])
    gt = hgu[:, :SHARED_DIM]
    up = hgu[:, SHARED_DIM:]
    moe = _dot((gt * _sigmoid(gt) * up).astype(BF16), wdn_ref[...])
    gate = gate_ref[...].astype(BF16).astype(F32)
    for kk in range(TOP_K):
        moe = moe + gate[:, kk:kk + 1] * yg_ref[kk].astype(F32)
    y = _layer_norm(alpha * x + moe, g_ref[...], b_ref[...])
    o_ref[...] = y
    ob_ref[...] = y.astype(ob_ref.dtype)


def _combine_call(x, xb, yg, gate, lp, alpha):
    m = x.shape[0]
    tm = _tile(m, 128)
    row = pl.BlockSpec((tm, D_MODEL), lambda i: (i, 0))
    vec = pl.BlockSpec((1, D_MODEL), lambda i: (0, 0))
    return pl.pallas_call(
        functools.partial(_combine_kernel, alpha=alpha),
        out_shape=(jax.ShapeDtypeStruct((m, D_MODEL), F32), jax.ShapeDtypeStruct((m, D_MODEL), BF16)),
        grid=(m // tm,),
        in_specs=[row, row,
                  pl.BlockSpec((TOP_K, tm, D_MODEL), lambda i: (0, i, 0)),
                  pl.BlockSpec((tm, LANES), lambda i: (i, 0)),
                  pl.BlockSpec((D_MODEL, 2 * SHARED_DIM), lambda i: (0, 0)),
                  pl.BlockSpec((SHARED_DIM, D_MODEL), lambda i: (0, 0)), vec, vec],
        out_specs=(row, row),
        compiler_params=_cparams(("parallel",)),
        name="moe_combine_ln2",
    )(x, xb, yg, gate, lp['shared_w_gate_up'], lp['shared_w_down'], lp['ln2_g'], lp['ln2_b'])


def _dispatch_plan(idx):
    n_tok = idx.shape[0]
    n_a = n_tok * TOP_K
    bm = MOE_BLOCK
    n_blocks = n_a // bm + N_EXPERTS
    e_flat = idx.reshape(n_a)
    order = jnp.argsort(e_flat, stable=True).astype(jnp.int32)
    rank = jnp.argsort(order).astype(jnp.int32)
    counts = jnp.sum(e_flat[:, None] == jnp.arange(N_EXPERTS, dtype=jnp.int32)[None, :], axis=0,
                     dtype=jnp.int32)
    padded = (counts + bm - 1) // bm * bm
    pad_end = jnp.cumsum(padded)
    pad_start = pad_end - padded
    start = jnp.cumsum(counts) - counts
    slot_of_assign = pad_start[e_flat] + rank - start[e_flat]
    blk_start = jnp.arange(n_blocks, dtype=jnp.int32) * bm
    block_expert = jnp.minimum(jnp.sum(pad_end[None, :] <= blk_start[:, None], axis=1, dtype=jnp.int32),
                               N_EXPERTS - 1)
    n_used = (pad_end[-1] // bm).astype(jnp.int32).reshape(1)
    last_e = block_expert[jnp.maximum(n_used[0] - 1, 0)]
    block_expert = jnp.where(jnp.arange(n_blocks) < n_used[0], block_expert, last_e)
    slot_e = jnp.repeat(block_expert, bm)
    j = jnp.arange(n_blocks * bm, dtype=jnp.int32) - pad_start[slot_e]
    src = order[jnp.clip(start[slot_e] + j, 0, n_a - 1)] // TOP_K
    slot_tok = jnp.where(j < counts[slot_e], src, 0)
    return slot_tok, slot_of_assign, block_expert, n_used


def _moe_call(x1, x1b, lp, alpha):
    idx_p, gate_p = _router_call(x1b, lp)
    idx = idx_p[:, :TOP_K]
    slot_tok, slot_of_assign, block_expert, n_used = _dispatch_plan(idx)
    n_tok = x1.shape[0]
    xs = x1b.at[slot_tok].get(mode='promise_in_bounds')
    ys = _expert_call(xs, block_expert, n_used, lp)
    slot_kmajor = slot_of_assign.reshape(n_tok, TOP_K).T.reshape(-1)
    yg = ys.at[slot_kmajor].get(mode='promise_in_bounds').reshape(TOP_K, n_tok, D_MODEL)
    return _combine_call(x1, x1b, yg, gate_p, lp, alpha)


def _prep_layer(l, w_in, rwkv_mu, rwkv_w0, rwkv_w2, rwkv_a0, rwkv_a2, rwkv_g2, rwkv_k_k, rwkv_k_a,
                rwkv_r_k, rwkv_ln_g, rwkv_ln_b, ret_ln_g, ret_ln_b, att_rel_bias, w_branch_rwkv,
                w_branch_ret, w_branch_att, w_out, ln1_g, ln1_b, router_w, router_bias,
                expert_w_gate_up, expert_w_down, shared_w_gate_up, shared_w_down, ln2_g, ln2_b):
    pad_c = RWKV_PROJ_PAD - RWKV_PROJ
    wi = w_in[l]
    wi = jnp.concatenate([wi[:, :RWKV_PROJ], jnp.zeros((D_MODEL, pad_c), wi.dtype), wi[:, RWKV_PROJ:]], axis=1)
    rowv = lambda a: a.reshape(1, -1).astype(F32)
    zrows = lambda n: jnp.zeros((n, RWKV_WIDTH), F32)
    g_rows = RWKV_PROJ_PAD - 3 * RWKV_WIDTH - LANES
    return {
        'w_in': wi.astype(BF16),
        'rwkv_mu': jnp.pad(rowv(rwkv_mu[l]), ((0, 0), (0, pad_c))),
        'rwkv_w0': rowv(rwkv_w0[l]),
        'rwkv_w2': jnp.concatenate([rwkv_w2[l], zrows(RWKV_A_LORA)], 0).astype(BF16),
        'rwkv_a0': rowv(rwkv_a0[l]),
        'rwkv_a2': jnp.concatenate([zrows(RWKV_DECAY_LORA), rwkv_a2[l]], 0).astype(BF16),
        'rwkv_g2': jnp.concatenate([rwkv_g2[l], zrows(g_rows - RWKV_GATE_LORA)], 0).astype(BF16),
        'rwkv_k_k': rowv(rwkv_k_k[l]), 'rwkv_k_a': rowv(rwkv_k_a[l]), 'rwkv_r_k': rowv(rwkv_r_k[l]),
        'rwkv_ln_g': rowv(rwkv_ln_g[l]), 'rwkv_ln_b': rowv(rwkv_ln_b[l]),
        'ret_ln_g': rowv(ret_ln_g[l]), 'ret_ln_b': rowv(ret_ln_b[l]),
        'att_bias': _att_bias(att_rel_bias[l]),
        'w_branch_rwkv': w_branch_rwkv[l].astype(BF16), 'w_branch_ret': w_branch_ret[l].astype(BF16),
        'w_branch_att': w_branch_att[l].astype(BF16), 'w_out': w_out[l].astype(BF16),
        'ln1_g': rowv(ln1_g[l]), 'ln1_b': rowv(ln1_b[l]),
        'router_w': jnp.pad(router_w[l], ((0, 0), (0, LANES - N_EXPERTS))).astype(BF16),
        'router_bias': jnp.pad(rowv(router_bias[l]), ((0, 0), (0, LANES - N_EXPERTS))),
        'layer': l, 'expert_w_gate_up': expert_w_gate_up, 'expert_w_down': expert_w_down,
        'shared_w_gate_up': shared_w_gate_up[l].astype(BF16), 'shared_w_down': shared_w_down[l].astype(BF16),
        'ln2_g': rowv(ln2_g[l]), 'ln2_b': rowv(ln2_b[l]),
    }


def _pad_shift(shift):
    s = jnp.pad(shift.astype(F32), ((0, 0), (0, RWKV_PROJ_PAD - RWKV_PROJ)))
    return jnp.pad(s[:, None, :], ((0, 0), (0, 7), (0, 0)))


def kernel(x_prompt, x_sample, cache_attn_k, cache_attn_v, state_rwkv, state_rwkv_shift, state_ret, w_in, rwkv_mu, rwkv_w0, rwkv_w2, rwkv_a0, rwkv_a2, rwkv_g2, rwkv_k_k, rwkv_k_a, rwkv_r_k, rwkv_ln_g, rwkv_ln_b, ret_ln_g, ret_ln_b, att_rel_bias, w_branch_rwkv, w_branch_ret, w_branch_att, w_out, ln1_g, ln1_b, router_w, router_bias, expert_w_gate_up, expert_w_down, shared_w_gate_up, shared_w_down, ln2_g, ln2_b):
    n_bp, n_s, _ = x_prompt.shape
    n_bs, n_t, _ = x_sample.shape
    depth = w_in.shape[0]
    n_p = n_bp * n_s
    n_q = n_bs * n_t
    assert n_s % ATT_QBLK == 0 and n_t == CHUNK and n_p % ATT_QBLK == 0
    alpha = float((2 * depth) ** 0.25)
    weights = (w_in, rwkv_mu, rwkv_w0, rwkv_w2, rwkv_a0, rwkv_a2, rwkv_g2, rwkv_k_k, rwkv_k_a, rwkv_r_k,
               rwkv_ln_g, rwkv_ln_b, ret_ln_g, ret_ln_b, att_rel_bias, w_branch_rwkv, w_branch_ret,
               w_branch_att, w_out, ln1_g, ln1_b, router_w, router_bias, expert_w_gate_up, expert_w_down,
               shared_w_gate_up, shared_w_down, ln2_g, ln2_b)

    x = jnp.concatenate([x_prompt.reshape(n_p, D_MODEL), x_sample.reshape(n_q, D_MODEL)], axis=0)
    xb = x.astype(BF16)
    l_c = cache_attn_k.shape[2]
    assert l_c == ATT_PAST_ROWS
    new_p = [[], [], [], [], []]
    new_s = [[], [], [], [], []]
    zeros_shift = jnp.zeros((n_bp, 8, RWKV_PROJ_PAD), F32)
    zeros_rwkv = jnp.zeros((n_bp, RWKV_HEADS // 2, RWKV_HEAD_DIM, LANES), F32)
    zeros_ret = jnp.zeros((n_bp, RET_HEADS, RET_HEAD_DIM, RET_HEAD_DIM), F32)
    for l in range(depth):
        lp = _prep_layer(l, *weights)
        h = _matmul(xb, lp['w_in'])
        hq = h[n_p:]

        ya_p, rs_p, sh_p = _rwkv_call(h, 0, n_bp, n_s, zeros_shift, zeros_rwkv, lp)
        ya_s, rs_s, sh_s = _rwkv_call(h, n_p // CHUNK, n_bs, n_t, _pad_shift(state_rwkv_shift[l]),
                                      _rwkv_state_to_pairs(state_rwkv[l]), lp)
        yr_p, ts_p = _ret_call(h, 0, n_bp, n_s, 0, zeros_ret, lp)
        yr_s, ts_s = _ret_call(h, n_p // n_t, n_bs, n_t, PAST_LEN, state_ret[l], lp)
        cq = OFF_ATT // 256
        cols = (cq, cq + ATT_WIDTH // 256, cq + 2 * ATT_WIDTH // 256)
        yc_p = _att_call(h, h, h, cols, 0, n_bp, n_s // ATT_QBLK, 0, n_s // ATT_QBLK, lp['att_bias'])
        q_s = hq[:, OFF_ATT:OFF_ATT + ATT_WIDTH].reshape(n_bs, n_t, ATT_WIDTH)
        k_s = hq[:, OFF_ATT + ATT_WIDTH:OFF_ATT + 2 * ATT_WIDTH].reshape(n_bs, n_t, ATT_WIDTH)
        v_s = hq[:, OFF_ATT + 2 * ATT_WIDTH:OFF_ATT + 3 * ATT_WIDTH].reshape(n_bs, n_t, ATT_WIDTH)
        win_rows = 3 * ATT_QBLK
        lead = ATT_PAST_ROWS - l_c
        tail = win_rows - ATT_PAST_ROWS - n_t
        padrows = lambda a, lo, hi: jnp.pad(a, ((0, 0), (lo, hi), (0, 0))).reshape(n_bs * win_rows, ATT_WIDTH)
        q_w = padrows(q_s, ATT_PAST_ROWS, tail)
        k_w = padrows(jnp.concatenate([cache_attn_k[l].reshape(n_bs, l_c, ATT_WIDTH).astype(F32), k_s], 1), lead, tail)
        v_w = padrows(jnp.concatenate([cache_attn_v[l].reshape(n_bs, l_c, ATT_WIDTH).astype(F32), v_s], 1), lead, tail)
        yc_s = _att_call(q_w, k_w, v_w, (0, 0, 0), 0, n_bs, 3, 2, 1, lp['att_bias'])
        yc_s = yc_s.reshape(n_bs, ATT_QBLK, ATT_WIDTH)[:, :n_t].reshape(n_q, ATT_WIDTH)

        ya = jnp.concatenate([ya_p, ya_s], axis=0)
        yr = jnp.concatenate([yr_p, yr_s], axis=0)
        yc = jnp.concatenate([yc_p, yc_s], axis=0)
        merged = _merge_call(ya, yr, yc, h, lp)
        x1, x1b = _outproj_call(x, merged, lp, alpha)
        x, xb = _moe_call(x1, x1b, lp, alpha)

        keep = min(ATT_PAST_ROWS, n_s)
        kv_rows = jnp.stack([lax.slice(h, ((b + 1) * n_s - keep, OFF_ATT + ATT_WIDTH),
                                       ((b + 1) * n_s, OFF_ATT + 3 * ATT_WIDTH)) for b in range(n_bp)], 0)
        kp = kv_rows[:, :, :ATT_WIDTH]
        vp = kv_rows[:, :, ATT_WIDTH:]
        st_p = (kp.reshape(n_bp, keep, ATT_HEADS, ATT_HEAD_DIM), vp.reshape(n_bp, keep, ATT_HEADS, ATT_HEAD_DIM),
                _rwkv_state_from_pairs(rs_p), sh_p[:, 0, :RWKV_PROJ], ts_p)
        st_s = (k_s.reshape(n_bs, n_t, ATT_HEADS, ATT_HEAD_DIM), v_s.reshape(n_bs, n_t, ATT_HEADS, ATT_HEAD_DIM),
                _rwkv_state_from_pairs(rs_s), sh_s[:, 0, :RWKV_PROJ], ts_s)
        for lst, arr in zip(new_p, st_p):
            lst.append(arr)
        for lst, arr in zip(new_s, st_s):
            lst.append(arr)
    yp = x[:n_p].reshape(n_bp, n_s, D_MODEL)
    ys = x[n_p:].reshape(n_bs, n_t, D_MODEL)
    outs_p = [jnp.stack(a, 0) for a in new_p]
    outs_s = [jnp.stack(a, 0) for a in new_s]
    return (yp, ys, *outs_p, *outs_s)
```

```python
import functools
import math

import jax
import jax.numpy as jnp
import numpy as np
from jax import lax
from jax.experimental import pallas as pl
from jax.experimental.pallas import tpu as pltpu

F32 = jnp.float32
BF16 = jnp.bfloat16

D_MODEL = 2048
PAST_LEN = 1024
CHUNK = 64
RWKV_WIDTH = 1024
RWKV_HEAD_DIM = 64
RWKV_HEADS = 16
RWKV_DECAY_LORA = 64
RWKV_A_LORA = 64
RWKV_GATE_LORA = 160
RWKV_PROJ = 3 * RWKV_WIDTH + RWKV_DECAY_LORA + RWKV_A_LORA + RWKV_GATE_LORA
RWKV_PROJ_PAD = 3584
RWKV_GN_EPS = 64e-5
RET_HEAD_DIM = 128
RET_WIDTH = 1024
RET_HEADS = 8
RET_GN_EPS = 1e-5
ROPE_BASE = 10000.0
RET_CHUNK = 64
ATT_HEAD_DIM = 64
ATT_WIDTH = 1024
ATT_HEADS = 16
ATT_LEFT_CHUNKS = 8
ATT_PAST_ROWS = ATT_LEFT_CHUNKS * CHUNK
ATT_QBLK = 256
ATT_WIN = ATT_QBLK + ATT_PAST_ROWS
REL_CLIP = 128
OFF_RET = RWKV_PROJ_PAD
OFF_ATT = OFF_RET + 4 * RET_WIDTH
OFF_GATE = OFF_ATT + 3 * ATT_WIDTH
IN_PROJ_PAD = OFF_GATE + 3 * D_MODEL
N_EXPERTS = 64
TOP_K = 8
N_GROUPS = 8
TOPK_GROUPS = 4
EXPERT_DIM = 512
SHARED_DIM = 512
ROUTED_SCALE = 2.5
MOE_BLOCK = 512
LN_EPS = 1e-5
NEG_INF = -1e30
LANES = 128

VMEM_LIMIT = 56 * 1024 * 1024


def _cparams(sem):
    return pltpu.CompilerParams(dimension_semantics=sem, vmem_limit_bytes=VMEM_LIMIT)


def _tile(n, target, align=8):
    for t in range(min(n, target), 0, -1):
        if n % t == 0 and t % align == 0:
            return t
    return n


def _dot(a, b):
    return jnp.dot(a, b, preferred_element_type=F32)


def _dot_nt(a, b):
    return lax.dot_general(a, b, (((1,), (1,)), ((), ())), preferred_element_type=F32)


def _dot_tn(a, b):
    return lax.dot_general(a, b, (((0,), (0,)), ((), ())), preferred_element_type=F32)


def _sigmoid(x):
    return 1.0 / (1.0 + jnp.exp(-x))


def _split3(x):
    hi = x.astype(BF16)
    r1 = x - hi.astype(F32)
    mid = r1.astype(BF16)
    lo = (r1 - mid.astype(F32)).astype(BF16)
    return hi, mid, lo


def _mm_kernel(x_ref, w_ref, o_ref):
    o_ref[...] = _dot(x_ref[...], w_ref[...])


def _matmul(x, w, tm_target=1280, tn_target=768):
    m, k = x.shape
    n = w.shape[1]
    tm = _tile(m, tm_target)
    tn = _tile(n, tn_target, LANES)
    return pl.pallas_call(
        _mm_kernel,
        out_shape=jax.ShapeDtypeStruct((m, n), F32),
        grid=(m // tm, n // tn),
        in_specs=[pl.BlockSpec((tm, k), lambda i, j: (i, 0)),
                  pl.BlockSpec((k, tn), lambda i, j: (0, j))],
        out_specs=pl.BlockSpec((tm, tn), lambda i, j: (i, j)),
        compiler_params=_cparams(("parallel", "parallel")),
        name="in_proj",
    )(x, w)


def _rwkv_kernel(*refs, n_rows):
    p_refs = refs[:n_rows]
    (shift_ref, s0_ref, mu_ref, w0_ref, w2_ref, a0_ref, a2_ref, g2_ref, kk_ref, ka_ref, rk_ref,
     lng_ref, lnb_ref, e_ref, et_ref,
     y_ref, sout_ref, shout_ref,
     s_scr, sb_scr, prev_scr, w_scr, a_scr, r_scr, b_scr, k_scr, v_scr, yo_scr, g_scr) = refs[n_rows:]
    c = pl.program_id(1)
    n_c = pl.num_programs(1)
    C = p_refs[0].shape[0]
    n_pairs = RWKV_WIDTH // LANES
    hd = RWKV_HEAD_DIM
    ROW_W, ROW_A, ROW_R, ROW_B, ROW_K, ROW_V = range(6)

    @pl.when(c == 0)
    def _():
        s_scr[...] = s0_ref[...]
        sb_scr[...] = s0_ref[...].astype(BF16)
        prev_scr[...] = shift_ref[...]

    e_m = e_ref[...]
    et_m = et_ref[...]

    def headsum(x):
        s = None
        for limb in _split3(x):
            t = _dot(limb, e_m)
            s = t if s is None else s + t
        out = None
        for limb in _split3(s):
            t = _dot(limb, et_m)
            out = t if out is None else out + t
        return out

    w = RWKV_WIDTH
    for rr in range(n_rows):
        pf = p_refs[rr][...]
        row = lax.broadcasted_iota(jnp.int32, pf.shape, 0)
        prev = jnp.where(row == 0, prev_scr[rr, 0:1, :], pltpu.roll(pf, 1, 0))
        prev_scr[rr, 0:1, :] = pf[C - 1:C, :]
        px = pf + (prev - pf) * mu_ref[...]
        r = px[:, 0:w]
        k = px[:, w:2 * w]
        v = px[:, 2 * w:3 * w]
        lora = px[:, 3 * w:3 * w + LANES]
        xg = px[:, 3 * w + LANES:RWKV_PROJ_PAD]
        z = w0_ref[...] + _dot(jnp.tanh(lora).astype(BF16), w2_ref[...])
        w_log = -(jnp.maximum(-z, 0.0) + jnp.log1p(jnp.exp(-jnp.abs(z)))) - 0.5
        a = _sigmoid(a0_ref[...] + _dot(lora.astype(BF16), a2_ref[...]))
        g_scr[rr] = _dot(_sigmoid(xg).astype(BF16), g2_ref[...])
        kk = k * kk_ref[...]
        kk = kk / jnp.maximum(jnp.sqrt(headsum(kk * kk)), 1e-12)
        w_scr[rr] = jnp.exp(-jnp.exp(w_log))
        a_scr[rr] = -kk
        r_scr[rr] = r
        b_scr[rr] = kk * a
        k_scr[rr] = k * (1.0 + (a - 1.0) * ka_ref[...])
        v_scr[rr] = v
    yo_scr[...] = jnp.zeros(yo_scr.shape, F32)

    lane8 = lax.broadcasted_iota(jnp.int32, (8, LANES), 1)
    sub8 = lax.broadcasted_iota(jnp.int32, (8, LANES), 0)
    sub8h = sub8[:, 0:hd]
    hsel = (lane8 // hd) == sub8
    left_lo = (sub8 & 3) >= 2
    left_lo_h = left_lo[:, 0:hd]
    right_lo = (sub8 & 1) == 1
    right_head = (lane8 // hd) == (sub8 // 4)
    pairs = [(rr, p) for rr in range(n_rows) for p in range(n_pairs)]
    row_scrs = (w_scr, a_scr, r_scr, b_scr, k_scr, v_scr)

    def limbs(x):
        hi = x.astype(BF16).astype(F32)
        return hi, x - hi

    def step(t, carry):
        t0 = pl.multiple_of((t // 8) * 8, 8)
        tj = t - t0
        xs, sas = [], []
        for rr, p in pairs:
            ls = slice(p * LANES, (p + 1) * LANES)
            x = None
            for q, scr in enumerate(row_scrs):
                tile = pltpu.roll(scr[rr, pl.ds(t0, 8), ls], (q + 8 - tj) & 7, 0)
                x = tile if x is None else jnp.where(sub8 == q, tile, x)
            a_rows = jnp.where(hsel, x[ROW_A:ROW_A + 1, :], 0.0).astype(BF16)
            xs.append(x)
            sas.append(_dot_nt(a_rows, sb_scr[rr, p]))
        upds = []
        for x, sa2 in zip(xs, sas):
            x_hi, x_lo = limbs(x)
            s_hi, s_lo = limbs(sa2)
            l_sa = jnp.where(left_lo_h,
                             jnp.where(sub8h < 4, s_lo[0:1, :], s_lo[1:2, :]),
                             jnp.where(sub8h < 4, s_hi[0:1, :], s_hi[1:2, :]))
            v_sel = jnp.where(left_lo, x_lo[ROW_V:ROW_V + 1, :], x_hi[ROW_V:ROW_V + 1, :])
            l_v = jnp.where(sub8 < 4, v_sel, pltpu.roll(v_sel, hd, 1))[:, 0:hd]
            left = jnp.concatenate([l_sa, l_v], axis=0).astype(BF16)
            r_b = jnp.where(right_head, jnp.where(right_lo, x_lo[ROW_B:ROW_B + 1, :], x_hi[ROW_B:ROW_B + 1, :]), 0.0)
            r_k = jnp.where(right_head, jnp.where(right_lo, x_lo[ROW_K:ROW_K + 1, :], x_hi[ROW_K:ROW_K + 1, :]), 0.0)
            right = jnp.concatenate([r_b, r_k], axis=0).astype(BF16)
            upds.append(_dot_tn(left, right))
        sbs = []
        for (rr, p), x, upd in zip(pairs, xs, upds):
            s_new = s_scr[rr, p] * x[ROW_W:ROW_W + 1, :] + upd
            s_scr[rr, p] = s_new
            sb = s_new.astype(BF16)
            sb_scr[rr, p] = sb
            sbs.append(sb)
        for (rr, p), x, sb in zip(pairs, xs, sbs):
            ls = slice(p * LANES, (p + 1) * LANES)
            r_rows = jnp.where(hsel, x[ROW_R:ROW_R + 1, :], 0.0).astype(BF16)
            y2 = _dot_nt(r_rows, sb)
            y_row = jnp.concatenate([y2[0:1, :], y2[1:2, :]], axis=1)
            yo_scr[rr, pl.ds(t0, 8), ls] = jnp.where(sub8 == tj, y_row, yo_scr[rr, pl.ds(t0, 8), ls])
        return carry

    lax.fori_loop(0, C, step, 0)

    inv_n = 1.0 / hd
    for rr in range(n_rows):
        y = yo_scr[rr]
        mean = headsum(y) * inv_n
        d = y - mean
        var = headsum(d * d) * inv_n
        yn = d * lax.rsqrt(var + RWKV_GN_EPS) * lng_ref[...] + lnb_ref[...]
        bonus = headsum(r_scr[rr] * k_scr[rr] * rk_ref[...]) * v_scr[rr]
        y_ref[rr] = ((yn + bonus) * g_scr[rr]).astype(y_ref.dtype)

    @pl.when(c == n_c - 1)
    def _():
        sout_ref[...] = s_scr[...]
        shout_ref[...] = prev_scr[...]


def _rwkv_consts():
    lane = np.arange(RWKV_WIDTH)
    e = (lane[:, None] // RWKV_HEAD_DIM == np.arange(LANES)[None, :]).astype(np.float32)
    return jnp.asarray(e, BF16), jnp.asarray(e.T, BF16)


def _rwkv_call(h, row_blk0, n_b, n_t, shift0, s0_pairs, lp):
    C = CHUNK
    n_c = n_t // C
    n_rows = 4 if n_b % 4 == 0 else (2 if n_b % 2 == 0 else 1)
    n_pairs = RWKV_WIDTH // LANES
    consts = _rwkv_consts()
    full = lambda arr: pl.BlockSpec(arr.shape, lambda i, c: (0,) * arr.ndim)
    params = [lp['rwkv_mu'], lp['rwkv_w0'], lp['rwkv_w2'], lp['rwkv_a0'], lp['rwkv_a2'], lp['rwkv_g2'],
              lp['rwkv_k_k'], lp['rwkv_k_a'], lp['rwkv_r_k'], lp['rwkv_ln_g'], lp['rwkv_ln_b']]

    def pspec(rr):
        return pl.BlockSpec((C, RWKV_PROJ_PAD), lambda i, c: (row_blk0 + (i * n_rows + rr) * n_c + c, 0))
    state_spec = pl.BlockSpec((n_rows, n_pairs, RWKV_HEAD_DIM, LANES), lambda i, c: (i, 0, 0, 0))
    shift_spec = pl.BlockSpec((n_rows, 8, RWKV_PROJ_PAD), lambda i, c: (i, 0, 0))
    rows_scr = pltpu.VMEM((n_rows, C, RWKV_WIDTH), F32)
    y, s_out, sh_out = pl.pallas_call(
        functools.partial(_rwkv_kernel, n_rows=n_rows),
        out_shape=(jax.ShapeDtypeStruct((n_b, n_t, RWKV_WIDTH), BF16),
                   jax.ShapeDtypeStruct((n_b, n_pairs, RWKV_HEAD_DIM, LANES), F32),
                   jax.ShapeDtypeStruct((n_b, 8, RWKV_PROJ_PAD), F32)),
        grid=(n_b // n_rows, n_c),
        in_specs=[pspec(rr) for rr in range(n_rows)] + [shift_spec, state_spec]
                 + [full(a) for a in params] + [full(a) for a in consts],
        out_specs=(pl.BlockSpec((n_rows, C, RWKV_WIDTH), lambda i, c: (i, c, 0)), state_spec, shift_spec),
        scratch_shapes=[pltpu.VMEM((n_rows, n_pairs, RWKV_HEAD_DIM, LANES), F32),
                        pltpu.VMEM((n_rows, n_pairs, RWKV_HEAD_DIM, LANES), BF16),
                        pltpu.VMEM((n_rows, 8, RWKV_PROJ_PAD), F32)] + [rows_scr] * 8,
        compiler_params=_cparams(("parallel", "arbitrary")),
        name="rwkv7_scan",
    )(*([h] * n_rows), shift0, s0_pairs, *params, *consts)
    return y.reshape(n_b * n_t, RWKV_WIDTH), s_out, sh_out


def _rwkv_state_to_pairs(s):
    n_b = s.shape[0]
    s5 = s.astype(F32).reshape(n_b, RWKV_HEADS // 2, 2, RWKV_HEAD_DIM, RWKV_HEAD_DIM)
    return s5.transpose(0, 1, 3, 2, 4).reshape(n_b, RWKV_HEADS // 2, RWKV_HEAD_DIM, LANES)


def _rwkv_state_from_pairs(sp):
    n_b = sp.shape[0]
    s5 = sp.reshape(n_b, RWKV_HEADS // 2, RWKV_HEAD_DIM, 2, RWKV_HEAD_DIM)
    return s5.transpose(0, 1, 3, 2, 4).reshape(n_b, RWKV_HEADS, RWKV_HEAD_DIM, RWKV_HEAD_DIM)


def _ret_kernel(q_ref, k_ref, v_ref, g_ref, cc_ref, ss_ref, dec_ref, qd_ref, kd_ref, s0_ref,
                lng_ref, lnb_ref, y_ref, sout_ref, s_scr, *, chunk_decay):
    hh = pl.program_id(1)
    c = pl.program_id(2)
    n_c = pl.num_programs(2)
    n_h = s_scr.shape[0]
    d = RET_HEAD_DIM

    @pl.when(c == 0)
    def _():
        s_scr[...] = s0_ref[0]

    cc = cc_ref[...]
    ss = ss_ref[...]
    outs = []
    for j in range(n_h):
        sl = slice(j * d, (j + 1) * d)
        q = q_ref[:, sl]
        k = k_ref[:, sl]
        v = v_ref[:, sl].astype(BF16)
        qr = q * cc + pltpu.roll(q, d // 2, 1) * ss
        kr = (k * cc + pltpu.roll(k, d // 2, 1) * ss) * (d ** -0.5)
        qb = qr.astype(BF16)
        scores = _dot_nt(qb, kr.astype(BF16)) * dec_ref[j]
        inner = _dot(scores.astype(BF16), v)
        s_j = s_scr[j]
        cross = _dot((qr * qd_ref[j]).astype(BF16), s_j.astype(BF16))
        kv = _dot_tn((kr * kd_ref[j]).astype(BF16), v)
        cd = jnp.where(hh == 0, chunk_decay[j], chunk_decay[n_h + j])
        s_scr[j] = s_j * cd + kv
        y = inner + cross
        mu = jnp.mean(y, axis=-1, keepdims=True)
        yc = y - mu
        var = jnp.mean(yc * yc, axis=-1, keepdims=True)
        outs.append(yc * lax.rsqrt(var + RET_GN_EPS))
    yn = jnp.concatenate(outs, axis=1) * lng_ref[...] + lnb_ref[...]
    gt = g_ref[...]
    y_ref[...] = (gt * _sigmoid(gt) * yn).astype(y_ref.dtype)

    @pl.when(c == n_c - 1)
    def _():
        sout_ref[0] = s_scr[...]


def _ret_consts(C, pos0, n_t):
    f = np.float32
    log_gamma = np.log1p(-np.exp2(-5.0 - np.arange(RET_HEADS, dtype=f))).astype(f)
    idx = np.arange(C, dtype=f)
    diff = idx[:, None] - idx[None, :]
    dec = np.where(diff >= 0, np.exp(log_gamma[:, None, None] * np.maximum(diff, 0.0)), 0.0).astype(f)
    qd = np.exp(log_gamma[:, None] * (idx + 1.0)[None, :]).astype(f)
    kd = np.exp(log_gamma[:, None] * (C - 1.0 - idx)[None, :]).astype(f)
    qd = np.broadcast_to(qd[:, :, None], (RET_HEADS, C, RET_HEAD_DIM)).copy()
    kd = np.broadcast_to(kd[:, :, None], (RET_HEADS, C, RET_HEAD_DIM)).copy()
    cd = tuple(float(x) for x in np.exp(log_gamma * C))
    half = RET_HEAD_DIM // 2
    inv = (ROPE_BASE ** (-np.arange(half, dtype=f) / half)).astype(f)
    pos = (pos0 + np.arange(n_t)).astype(f)
    ang = pos[:, None] * inv[None, :]
    cos, sin = np.cos(ang).astype(f), np.sin(ang).astype(f)
    cc = np.concatenate([cos, cos], axis=1)
    ss = np.concatenate([-sin, sin], axis=1)
    return jnp.asarray(dec), jnp.asarray(qd), jnp.asarray(kd), cd, jnp.asarray(cc), jnp.asarray(ss)


def _ret_call(h, row_blk0, n_b, n_t, pos0, s0, lp):
    C = min(RET_CHUNK, n_t)
    n_c = n_t // C
    hw = 512
    n_hh = RET_WIDTH // hw
    hp = hw // RET_HEAD_DIM
    dec, qd, kd, cd, cc, ss = _ret_consts(C, pos0, n_t)
    col0 = OFF_RET // hw
    blk = lambda part: pl.BlockSpec(
        (C, hw), lambda b, hh, c: (row_blk0 + b * n_c + c, col0 + part * n_hh + hh))
    y, s_out = pl.pallas_call(
        functools.partial(_ret_kernel, chunk_decay=cd),
        out_shape=(jax.ShapeDtypeStruct((n_b * n_t, RET_WIDTH), BF16),
                   jax.ShapeDtypeStruct((n_b, RET_HEADS, RET_HEAD_DIM, RET_HEAD_DIM), F32)),
        grid=(n_b, n_hh, n_c),
        in_specs=[blk(0), blk(1), blk(2), blk(3),
                  pl.BlockSpec((C, RET_HEAD_DIM), lambda b, hh, c: (c, 0)),
                  pl.BlockSpec((C, RET_HEAD_DIM), lambda b, hh, c: (c, 0)),
                  pl.BlockSpec((hp, C, C), lambda b, hh, c: (hh, 0, 0)),
                  pl.BlockSpec((hp, C, RET_HEAD_DIM), lambda b, hh, c: (hh, 0, 0)),
                  pl.BlockSpec((hp, C, RET_HEAD_DIM), lambda b, hh, c: (hh, 0, 0)),
                  pl.BlockSpec((1, hp, RET_HEAD_DIM, RET_HEAD_DIM), lambda b, hh, c: (b, hh, 0, 0)),
                  pl.BlockSpec((1, hw), lambda b, hh, c: (0, hh)),
                  pl.BlockSpec((1, hw), lambda b, hh, c: (0, hh))],
        out_specs=(pl.BlockSpec((C, hw), lambda b, hh, c: (b * n_c + c, hh)),
                   pl.BlockSpec((1, hp, RET_HEAD_DIM, RET_HEAD_DIM), lambda b, hh, c: (b, hh, 0, 0))),
        scratch_shapes=[pltpu.VMEM((hp, RET_HEAD_DIM, RET_HEAD_DIM), F32)],
        compiler_params=_cparams(("parallel", "parallel", "arbitrary")),
        name="retention_chunk",
    )(h, h, h, h, cc, ss, dec, qd, kd, s0.astype(F32), lp['ret_ln_g'], lp['ret_ln_b'])
    return y, s_out


def _att_kernel(q_ref, k0_ref, k1_ref, k2_ref, v0_ref, v1_ref, v2_ref, bias_ref, o_ref, *, blk0):
    blk = pl.program_id(2) + blk0
    hd = ATT_HEAD_DIM
    n_pairs = q_ref.shape[1] // LANES
    lane = lax.broadcasted_iota(jnp.int32, (1, LANES), 1)
    k_refs = (k0_ref, k1_ref, k2_ref)
    v_refs = (v0_ref, v1_ref, v2_ref)
    valid = (blk >= 2, blk >= 1, None)
    outs = []
    for p in range(n_pairs):
        sl = slice(p * LANES, (p + 1) * LANES)
        q = q_ref[:, sl]
        ks = [kr[:, sl].astype(BF16) for kr in k_refs]
        vs = [vr[:, sl].astype(BF16) for vr in v_refs]
        o_heads = []
        for j in range(2):
            qm = jnp.where((lane // hd) == j, q, 0.0).astype(BF16)
            parts = []
            for kb in range(3):
                s = _dot_nt(qm, ks[kb]) * (hd ** -0.5)
                s = s + bias_ref[2 * p + j, :, kb * ATT_QBLK:(kb + 1) * ATT_QBLK]
                if valid[kb] is not None:
                    s = jnp.where(valid[kb], s, NEG_INF)
                parts.append(s)
            s = jnp.concatenate(parts, axis=1)
            m = jnp.max(s, axis=-1, keepdims=True)
            e = jnp.exp(s - m)
            pr = (e / jnp.sum(e, axis=-1, keepdims=True)).astype(BF16)
            o = _dot(pr[:, 0:ATT_QBLK], vs[0])
            o = o + _dot(pr[:, ATT_QBLK:2 * ATT_QBLK], vs[1])
            o = o + _dot(pr[:, 2 * ATT_QBLK:3 * ATT_QBLK], vs[2])
            o_heads.append(o)
        outs.append(jnp.where((lane // hd) == 0, o_heads[0], o_heads[1]))
    o_ref[...] = jnp.concatenate(outs, axis=1).astype(o_ref.dtype)


def _att_bias(table):
    n_h = table.shape[0]
    r = np.arange(ATT_QBLK)[:, None]
    w = np.arange(ATT_WIN)[None, :]
    lo = CHUNK * (r // CHUNK)
    band = (w >= lo) & (w < lo + ATT_PAST_ROWS + CHUNK)
    period = ATT_QBLK + ATT_WIN + 1
    j = np.arange(period)
    d_idx = np.clip(ATT_QBLK - j + ATT_PAST_ROWS, -REL_CLIP, REL_CLIP) + REL_CLIP
    d = table.astype(F32)[:, d_idx]
    skew = jnp.tile(d, (1, ATT_QBLK))[:, :ATT_QBLK * (period - 1)].reshape(n_h, ATT_QBLK, period - 1)
    b = skew[:, :, ATT_QBLK:ATT_QBLK + ATT_WIN]
    return jnp.where(jnp.asarray(band)[None], b, NEG_INF)


def _att_call(q_arr, k_arr, v_arr, cols, row_blk0, n_b, n_blk_batch, blk0, n_blk, bias):
    gw = 256
    n_hg = ATT_WIDTH // gw
    hpg = gw // ATT_HEAD_DIM
    qc, kc, vc = cols
    rowb = lambda b, i: row_blk0 + b * n_blk_batch + i
    qspec = pl.BlockSpec((ATT_QBLK, gw), lambda g, b, i: (rowb(b, i + blk0), qc + g))

    def kvspec(col, back):
        return pl.BlockSpec((ATT_QBLK, gw),
                            lambda g, b, i: (rowb(b, jnp.maximum(i + blk0 - back, 0)), col + g))
    return pl.pallas_call(
        functools.partial(_att_kernel, blk0=blk0),
        out_shape=jax.ShapeDtypeStruct((n_b * n_blk * ATT_QBLK, ATT_WIDTH), BF16),
        grid=(n_hg, n_b, n_blk),
        in_specs=[qspec, kvspec(kc, 2), kvspec(kc, 1), kvspec(kc, 0),
                  kvspec(vc, 2), kvspec(vc, 1), kvspec(vc, 0),
                  pl.BlockSpec((hpg, ATT_QBLK, ATT_WIN), lambda g, b, i: (g, 0, 0))],
        out_specs=pl.BlockSpec((ATT_QBLK, gw), lambda g, b, i: (b * n_blk + i, g)),
        compiler_params=_cparams(("parallel", "parallel", "arbitrary")),
        name="band_attention",
    )(q_arr, k_arr, k_arr, k_arr, v_arr, v_arr, v_arr, bias)


def _merge_kernel(ya_ref, yr_ref, yc_ref, ga_ref, gr_ref, gc_ref, wa_ref, wr_ref, wc_ref, o_ref):
    m = _sigmoid(ga_ref[...]) * _dot(ya_ref[...], wa_ref[...])
    m = m + _sigmoid(gr_ref[...]) * _dot(yr_ref[...], wr_ref[...])
    m = m + _sigmoid(gc_ref[...]) * _dot(yc_ref[...], wc_ref[...])
    o_ref[...] = m.astype(o_ref.dtype)


def _merge_call(ya, yr, yc, h, lp):
    m = ya.shape[0]
    tm = _tile(m, 512)
    tn = 512
    n_n = D_MODEL // tn
    g0 = OFF_GATE // tn
    yspec = pl.BlockSpec((tm, RWKV_WIDTH), lambda i, j: (i, 0))
    wspec = pl.BlockSpec((RWKV_WIDTH, tn), lambda i, j: (0, j))
    gspec = lambda part: pl.BlockSpec((tm, tn), lambda i, j: (i, g0 + part * n_n + j))
    return pl.pallas_call(
        _merge_kernel,
        out_shape=jax.ShapeDtypeStruct((m, D_MODEL), BF16),
        grid=(m // tm, n_n),
        in_specs=[yspec, yspec, yspec, gspec(0), gspec(1), gspec(2), wspec, wspec, wspec],
        out_specs=pl.BlockSpec((tm, tn), lambda i, j: (i, j)),
        compiler_params=_cparams(("parallel", "parallel")),
        name="branch_merge",
    )(ya, yr, yc, h, h, h, lp['w_branch_rwkv'], lp['w_branch_ret'], lp['w_branch_att'])


def _layer_norm(z, g, b):
    mu = jnp.mean(z, axis=-1, keepdims=True)
    zc = z - mu
    var = jnp.mean(zc * zc, axis=-1, keepdims=True)
    return zc * lax.rsqrt(var + LN_EPS) * g + b


def _outproj_kernel(x_ref, m_ref, w_ref, g_ref, b_ref, o_ref, ob_ref, *, alpha):
    z = alpha * x_ref[...] + _dot(m_ref[...], w_ref[...])
    y = _layer_norm(z, g_ref[...], b_ref[...])
    o_ref[...] = y
    ob_ref[...] = y.astype(ob_ref.dtype)


def _outproj_call(x, merged, lp, alpha):
    m = x.shape[0]
    tm = _tile(m, 256)
    row = pl.BlockSpec((tm, D_MODEL), lambda i: (i, 0))
    vec = pl.BlockSpec((1, D_MODEL), lambda i: (0, 0))
    return pl.pallas_call(
        functools.partial(_outproj_kernel, alpha=alpha),
        out_shape=(jax.ShapeDtypeStruct((m, D_MODEL), F32), jax.ShapeDtypeStruct((m, D_MODEL), BF16)),
        grid=(m // tm,),
        in_specs=[row, row, pl.BlockSpec((D_MODEL, D_MODEL), lambda i: (0, 0)), vec, vec],
        out_specs=(row, row),
        compiler_params=_cparams(("parallel",)),
        name="out_proj_ln1",
    )(x, merged, lp['w_out'], lp['ln1_g'], lp['ln1_b'])


def _router_kernel(x_ref, w_ref, b_ref, idx_ref, gate_ref):
    x = x_ref[...]
    logits = _dot(x, w_ref[...])
    scores = _sigmoid(logits)
    tm = x.shape[0]
    lane_i = lax.broadcasted_iota(jnp.int32, (tm, LANES), 1)
    lane = lane_i.astype(F32)
    real = lane_i < N_EXPERTS
    sel = jnp.where(real, scores + b_ref[...], NEG_INF)
    per_group = N_EXPERTS // N_GROUPS
    grp = (lane_i // per_group).astype(F32)

    def first_argmax(vals):
        m = jnp.max(vals, axis=-1, keepdims=True)
        i = jnp.min(jnp.where(vals == m, lane, float(LANES)), axis=-1, keepdims=True)
        return m, i

    gscore = jnp.full((tm, LANES), NEG_INF, F32)
    for gidx in range(N_GROUPS):
        in_g = grp == gidx
        vals = jnp.where(in_g, sel, -jnp.inf)
        m1, i1 = first_argmax(vals)
        m2 = jnp.max(jnp.where(lane == i1, -jnp.inf, vals), axis=-1, keepdims=True)
        gscore = jnp.where(in_g, m1 + m2, gscore)
    chosen = jnp.zeros((tm, LANES), jnp.bool_)
    cand = jnp.where(real, gscore, -jnp.inf)
    for _ in range(TOPK_GROUPS):
        _, i = first_argmax(cand)
        pick = grp == jnp.floor(i * (1.0 / per_group))
        chosen = jnp.logical_or(chosen, pick)
        cand = jnp.where(pick, -jnp.inf, cand)
    cand = jnp.where(real, jnp.where(chosen, sel, NEG_INF), -jnp.inf)
    idx_out = jnp.zeros((tm, LANES), F32)
    w_out = jnp.zeros((tm, LANES), F32)
    for kk in range(TOP_K):
        _, i = first_argmax(cand)
        hit = lane == i
        wk = jnp.sum(jnp.where(hit, scores, 0.0), axis=-1, keepdims=True)
        idx_out = jnp.where(lane == kk, i, idx_out)
        w_out = jnp.where(lane == kk, wk, w_out)
        cand = jnp.where(hit, -jnp.inf, cand)
    total = jnp.sum(w_out, axis=-1, keepdims=True)
    idx_ref[...] = idx_out.astype(jnp.int32)
    gate_ref[...] = w_out / total * ROUTED_SCALE


def _router_call(x, lp):
    m = x.shape[0]
    tm = _tile(m, 512)
    row = pl.BlockSpec((tm, LANES), lambda i: (i, 0))
    return pl.pallas_call(
        _router_kernel,
        out_shape=(jax.ShapeDtypeStruct((m, LANES), jnp.int32), jax.ShapeDtypeStruct((m, LANES), F32)),
        grid=(m // tm,),
        in_specs=[pl.BlockSpec((tm, D_MODEL), lambda i: (i, 0)),
                  pl.BlockSpec((D_MODEL, LANES), lambda i: (0, 0)),
                  pl.BlockSpec((1, LANES), lambda i: (0, 0))],
        out_specs=(row, row),
        compiler_params=_cparams(("parallel",)),
        name="router_topk",
    )(x, lp['router_w'], lp['router_bias'])


def _expert_kernel(be_ref, nu_ref, x_ref, wgu_ref, wdn_ref, o_ref, wgu_scr, wdn_scr):
    i = pl.program_id(0)

    @pl.when(jnp.logical_or(i == 0, be_ref[i] != be_ref[jnp.maximum(i - 1, 0)]))
    def _():
        wgu_scr[...] = wgu_ref[0, 0].astype(BF16)
        wdn_scr[...] = wdn_ref[0, 0].astype(BF16)

    @pl.when(i < nu_ref[0])
    def _():
        hgu = _dot(x_ref[...], wgu_scr[...])
        gt = hgu[:, :EXPERT_DIM]
        up = hgu[:, EXPERT_DIM:]
        act = (gt * _sigmoid(gt) * up).astype(BF16)
        o_ref[...] = _dot(act, wdn_scr[...]).astype(o_ref.dtype)

    @pl.when(i >= nu_ref[0])
    def _():
        o_ref[...] = jnp.zeros_like(o_ref)


def _expert_call(xs, block_expert, n_used, lp):
    n_slots = xs.shape[0]
    bm = MOE_BLOCK
    n_blocks = n_slots // bm
    layer = lp['layer']
    grid_spec = pltpu.PrefetchScalarGridSpec(
        num_scalar_prefetch=2,
        grid=(n_blocks,),
        in_specs=[pl.BlockSpec((bm, D_MODEL), lambda i, be, nu: (i, 0)),
                  pl.BlockSpec((1, 1, D_MODEL, 2 * EXPERT_DIM), lambda i, be, nu: (layer, be[i], 0, 0)),
                  pl.BlockSpec((1, 1, EXPERT_DIM, D_MODEL), lambda i, be, nu: (layer, be[i], 0, 0))],
        out_specs=pl.BlockSpec((bm, D_MODEL), lambda i, be, nu: (i, 0)),
        scratch_shapes=[pltpu.VMEM((D_MODEL, 2 * EXPERT_DIM), BF16), pltpu.VMEM((EXPERT_DIM, D_MODEL), BF16)],
    )
    return pl.pallas_call(
        _expert_kernel,
        out_shape=jax.ShapeDtypeStruct((n_slots, D_MODEL), BF16),
        grid_spec=grid_spec,
        compiler_params=_cparams(("arbitrary",)),
        name="routed_experts",
    )(block_expert, n_used, xs, lp['expert_w_gate_up'], lp['expert_w_down'])


def _combine_kernel(x_ref, xb_ref, yg_ref, gate_ref, wgu_ref, wdn_ref, g_ref, b_ref, o_ref, ob_ref, *, alpha):
    x = x_ref[...]
    hgu = _dot(xb_ref[...], wgu_ref[...])
    gt = hgu[:, :SHARED_DIM]
    up = hgu[:, SHARED_DIM:]
    moe = _dot((gt * _sigmoid(gt) * up).astype(BF16), wdn_ref[...])
    gate = gate_ref[...].astype(BF16).astype(F32)
    for kk in range(TOP_K):
        moe = moe + gate[:, kk:kk + 1] * yg_ref[kk].astype(F32)
    y = _layer_norm(alpha * x + moe, g_ref[...], b_ref[...])
    o_ref[...] = y
    ob_ref[...] = y.astype(ob_ref.dtype)


def _combine_call(x, xb, yg, gate, lp, alpha):
    m = x.shape[0]
    tm = _tile(m, 128)
    row = pl.BlockSpec((tm, D_MODEL), lambda i: (i, 0))
    vec = pl.BlockSpec((1, D_MODEL), lambda i: (0, 0))
    return pl.pallas_call(
        functools.partial(_combine_kernel, alpha=alpha),
        out_shape=(jax.ShapeDtypeStruct((m, D_MODEL), F32), jax.ShapeDtypeStruct((m, D_MODEL), BF16)),
        grid=(m // tm,),
        in_specs=[row, row,
                  pl.BlockSpec((TOP_K, tm, D_MODEL), lambda i: (0, i, 0)),
                  pl.BlockSpec((tm, LANES), lambda i: (i, 0)),
                  pl.BlockSpec((D_MODEL, 2 * SHARED_DIM), lambda i: (0, 0)),
                  pl.BlockSpec((SHARED_DIM, D_MODEL), lambda i: (0, 0)), vec, vec],
        out_specs=(row, row),
        compiler_params=_cparams(("parallel",)),
        name="moe_combine_ln2",
    )(x, xb, yg, gate, lp['shared_w_gate_up'], lp['shared_w_down'], lp['ln2_g'], lp['ln2_b'])


def _dispatch_plan(idx):
    n_tok = idx.shape[0]
    n_a = n_tok * TOP_K
    bm = MOE_BLOCK
    n_blocks = n_a // bm + N_EXPERTS
    e_flat = idx.reshape(n_a)
    order = jnp.argsort(e_flat, stable=True).astype(jnp.int32)
    rank = jnp.argsort(order).astype(jnp.int32)
    counts = jnp.sum(e_flat[:, None] == jnp.arange(N_EXPERTS, dtype=jnp.int32)[None, :], axis=0,
                     dtype=jnp.int32)
    padded = (counts + bm - 1) // bm * bm
    pad_end = jnp.cumsum(padded)
    pad_start = pad_end - padded
    start = jnp.cumsum(counts) - counts
    slot_of_assign = pad_start[e_flat] + rank - start[e_flat]
    blk_start = jnp.arange(n_blocks, dtype=jnp.int32) * bm
    block_expert = jnp.minimum(jnp.sum(pad_end[None, :] <= blk_start[:, None], axis=1, dtype=jnp.int32),
                               N_EXPERTS - 1)
    n_used = (pad_end[-1] // bm).astype(jnp.int32).reshape(1)
    last_e = block_expert[jnp.maximum(n_used[0] - 1, 0)]
    block_expert = jnp.where(jnp.arange(n_blocks) < n_used[0], block_expert, last_e)
    slot_e = jnp.repeat(block_expert, bm)
    j = jnp.arange(n_blocks * bm, dtype=jnp.int32) - pad_start[slot_e]
    src = order[jnp.clip(start[slot_e] + j, 0, n_a - 1)] // TOP_K
    slot_tok = jnp.where(j < counts[slot_e], src, 0)
    return slot_tok, slot_of_assign, block_expert, n_used


def _moe_call(x1, x1b, lp, alpha):
    idx_p, gate_p = _router_call(x1b, lp)
    idx = idx_p[:, :TOP_K]
    slot_tok, slot_of_assign, block_expert, n_used = _dispatch_plan(idx)
    n_tok = x1.shape[0]
    xs = x1b.at[slot_tok].get(mode='promise_in_bounds')
    ys = _expert_call(xs, block_expert, n_used, lp)
    slot_kmajor = slot_of_assign.reshape(n_tok, TOP_K).T.reshape(-1)
    yg = ys.at[slot_kmajor].get(mode='promise_in_bounds').reshape(TOP_K, n_tok, D_MODEL)
    return _combine_call(x1, x1b, yg, gate_p, lp, alpha)


def _prep_layer(l, w_in, rwkv_mu, rwkv_w0, rwkv_w2, rwkv_a0, rwkv_a2, rwkv_g2, rwkv_k_k, rwkv_k_a,
                rwkv_r_k, rwkv_ln_g, rwkv_ln_b, ret_ln_g, ret_ln_b, att_rel_bias, w_branch_rwkv,
                w_branch_ret, w_branch_att, w_out, ln1_g, ln1_b, router_w, router_bias,
                expert_w_gate_up, expert_w_down, shared_w_gate_up, shared_w_down, ln2_g, ln2_b):
    pad_c = RWKV_PROJ_PAD - RWKV_PROJ
    wi = w_in[l]
    wi = jnp.concatenate([wi[:, :RWKV_PROJ], jnp.zeros((D_MODEL, pad_c), wi.dtype), wi[:, RWKV_PROJ:]], axis=1)
    rowv = lambda a: a.reshape(1, -1).astype(F32)
    zrows = lambda n: jnp.zeros((n, RWKV_WIDTH), F32)
    g_rows = RWKV_PROJ_PAD - 3 * RWKV_WIDTH - LANES
    return {
        'w_in': wi.astype(BF16),
        'rwkv_mu': jnp.pad(rowv(rwkv_mu[l]), ((0, 0), (0, pad_c))),
        'rwkv_w0': rowv(rwkv_w0[l]),
        'rwkv_w2': jnp.concatenate([rwkv_w2[l], zrows(RWKV_A_LORA)], 0).astype(BF16),
        'rwkv_a0': rowv(rwkv_a0[l]),
        'rwkv_a2': jnp.concatenate([zrows(RWKV_DECAY_LORA), rwkv_a2[l]], 0).astype(BF16),
        'rwkv_g2': jnp.concatenate([rwkv_g2[l], zrows(g_rows - RWKV_GATE_LORA)], 0).astype(BF16),
        'rwkv_k_k': rowv(rwkv_k_k[l]), 'rwkv_k_a': rowv(rwkv_k_a[l]), 'rwkv_r_k': rowv(rwkv_r_k[l]),
        'rwkv_ln_g': rowv(rwkv_ln_g[l]), 'rwkv_ln_b': rowv(rwkv_ln_b[l]),
        'ret_ln_g': rowv(ret_ln_g[l]), 'ret_ln_b': rowv(ret_ln_b[l]),
        'att_bias': _att_bias(att_rel_bias[l]),
        'w_branch_rwkv': w_branch_rwkv[l].astype(BF16), 'w_branch_ret': w_branch_ret[l].astype(BF16),
        'w_branch_att': w_branch_att[l].astype(BF16), 'w_out': w_out[l].astype(BF16),
        'ln1_g': rowv(ln1_g[l]), 'ln1_b': rowv(ln1_b[l]),
        'router_w': jnp.pad(router_w[l], ((0, 0), (0, LANES - N_EXPERTS))).astype(BF16),
        'router_bias': jnp.pad(rowv(router_bias[l]), ((0, 0), (0, LANES - N_EXPERTS))),
        'layer': l, 'expert_w_gate_up': expert_w_gate_up, 'expert_w_down': expert_w_down,
        'shared_w_gate_up': shared_w_gate_up[l].astype(BF16), 'shared_w_down': shared_w_down[l].astype(BF16),
        'ln2_g': rowv(ln2_g[l]), 'ln2_b': rowv(ln2_b[l]),
    }


def _pad_shift(shift):
    s = jnp.pad(shift.astype(F32), ((0, 0), (0, RWKV_PROJ_PAD - RWKV_PROJ)))
    return jnp.pad(s[:, None, :], ((0, 0), (0, 7), (0, 0)))


def kernel(x_prompt, x_sample, cache_attn_k, cache_attn_v, state_rwkv, state_rwkv_shift, state_ret, w_in, rwkv_mu, rwkv_w0, rwkv_w2, rwkv_a0, rwkv_a2, rwkv_g2, rwkv_k_k, rwkv_k_a, rwkv_r_k, rwkv_ln_g, rwkv_ln_b, ret_ln_g, ret_ln_b, att_rel_bias, w_branch_rwkv, w_branch_ret, w_branch_att, w_out, ln1_g, ln1_b, router_w, router_bias, expert_w_gate_up, expert_w_down, shared_w_gate_up, shared_w_down, ln2_g, ln2_b):
    n_bp, n_s, _ = x_prompt.shape
    n_bs, n_t, _ = x_sample.shape
    depth = w_in.shape[0]
    n_p = n_bp * n_s
    n_q = n_bs * n_t
    assert n_s % ATT_QBLK == 0 and n_t == CHUNK and n_p % ATT_QBLK == 0
    alpha = float((2 * depth) ** 0.25)
    weights = (w_in, rwkv_mu, rwkv_w0, rwkv_w2, rwkv_a0, rwkv_a2, rwkv_g2, rwkv_k_k, rwkv_k_a, rwkv_r_k,
               rwkv_ln_g, rwkv_ln_b, ret_ln_g, ret_ln_b, att_rel_bias, w_branch_rwkv, w_branch_ret,
               w_branch_att, w_out, ln1_g, ln1_b, router_w, router_bias, expert_w_gate_up, expert_w_down,
               shared_w_gate_up, shared_w_down, ln2_g, ln2_b)

    x = jnp.concatenate([x_prompt.reshape(n_p, D_MODEL), x_sample.reshape(n_q, D_MODEL)], axis=0)
    xb = x.astype(BF16)
    l_c = cache_attn_k.shape[2]
    assert l_c == ATT_PAST_ROWS
    new_p = [[], [], [], [], []]
    new_s = [[], [], [], [], []]
    zeros_shift = jnp.zeros((n_bp, 8, RWKV_PROJ_PAD), F32)
    zeros_rwkv = jnp.zeros((n_bp, RWKV_HEADS // 2, RWKV_HEAD_DIM, LANES), F32)
    zeros_ret = jnp.zeros((n_bp, RET_HEADS, RET_HEAD_DIM, RET_HEAD_DIM), F32)
    for l in range(depth):
        lp = _prep_layer(l, *weights)
        h = _matmul(xb, lp['w_in'])
        hq = h[n_p:]

        ya_p, rs_p, sh_p = _rwkv_call(h, 0, n_bp, n_s, zeros_shift, zeros_rwkv, lp)
        ya_s, rs_s, sh_s = _rwkv_call(h, n_p // CHUNK, n_bs, n_t, _pad_shift(state_rwkv_shift[l]),
                                      _rwkv_state_to_pairs(state_rwkv[l]), lp)
        yr_p, ts_p = _ret_call(h, 0, n_bp, n_s, 0, zeros_ret, lp)
        yr_s, ts_s = _ret_call(h, n_p // n_t, n_bs, n_t, PAST_LEN, state_ret[l], lp)
        cq = OFF_ATT // 256
        cols = (cq, cq + ATT_WIDTH // 256, cq + 2 * ATT_WIDTH // 256)
        yc_p = _att_call(h, h, h, cols, 0, n_bp, n_s // ATT_QBLK, 0, n_s // ATT_QBLK, lp['att_bias'])
        q_s = hq[:, OFF_ATT:OFF_ATT + ATT_WIDTH].reshape(n_bs, n_t, ATT_WIDTH)
        k_s = hq[:, OFF_ATT + ATT_WIDTH:OFF_ATT + 2 * ATT_WIDTH].reshape(n_bs, n_t, ATT_WIDTH)
        v_s = hq[:, OFF_ATT + 2 * ATT_WIDTH:OFF_ATT + 3 * ATT_WIDTH].reshape(n_bs, n_t, ATT_WIDTH)
        win_rows = 3 * ATT_QBLK
        lead = ATT_PAST_ROWS - l_c
        tail = win_rows - ATT_PAST_ROWS - n_t
        padrows = lambda a, lo, hi: jnp.pad(a, ((0, 0), (lo, hi), (0, 0))).reshape(n_bs * win_rows, ATT_WIDTH)
        q_w = padrows(q_s, ATT_PAST_ROWS, tail)
        k_w = padrows(jnp.concatenate([cache_attn_k[l].reshape(n_bs, l_c, ATT_WIDTH).astype(F32), k_s], 1), lead, tail)
        v_w = padrows(jnp.concatenate([cache_attn_v[l].reshape(n_bs, l_c, ATT_WIDTH).astype(F32), v_s], 1), lead, tail)
        yc_s = _att_call(q_w, k_w, v_w, (0, 0, 0), 0, n_bs, 3, 2, 1, lp['att_bias'])
        yc_s = yc_s.reshape(n_bs, ATT_QBLK, ATT_WIDTH)[:, :n_t].reshape(n_q, ATT_WIDTH)

        ya = jnp.concatenate([ya_p, ya_s], axis=0)
        yr = jnp.concatenate([yr_p, yr_s], axis=0)
        yc = jnp.concatenate([yc_p, yc_s], axis=0)
        merged = _merge_call(ya, yr, yc, h, lp)
        x1, x1b = _outproj_call(x, merged, lp, alpha)
        x, xb = _moe_call(x1, x1b, lp, alpha)

        keep = min(ATT_PAST_ROWS, n_s)
        kv_rows = jnp.stack([lax.slice(h, ((b + 1) * n_s - keep, OFF_ATT + ATT_WIDTH),
                                       ((b + 1) * n_s, OFF_ATT + 3 * ATT_WIDTH)) for b in range(n_bp)], 0)
        kp = kv_rows[:, :, :ATT_WIDTH]
        vp = kv_rows[:, :, ATT_WIDTH:]
        st_p = (kp.reshape(n_bp, keep, ATT_HEADS, ATT_HEAD_DIM), vp.reshape(n_bp, keep, ATT_HEADS, ATT_HEAD_DIM),
                _rwkv_state_from_pairs(rs_p), sh_p[:, 0, :RWKV_PROJ], ts_p)
        st_s = (k_s.reshape(n_bs, n_t, ATT_HEADS, ATT_HEAD_DIM), v_s.reshape(n_bs, n_t, ATT_HEADS, ATT_HEAD_DIM),
                _rwkv_state_from_pairs(rs_s), sh_s[:, 0, :RWKV_PROJ], ts_s)
        for lst, arr in zip(new_p, st_p):
            lst.append(arr)
        for lst, arr in zip(new_s, st_s):
            lst.append(arr)
    yp = x[:n_p].reshape(n_bp, n_s, D_MODEL)
    ys = x[n_p:].reshape(n_bs, n_t, D_MODEL)
    outs_p = [jnp.stack(a, 0) for a in new_p]
    outs_s = [jnp.stack(a, 0) for a in new_s]
    return (yp, ys, *outs_p, *outs_s)
```

```python
import functools
import math

import jax
import jax.numpy as jnp
import numpy as np
from jax import lax
from jax.experimental import pallas as pl
from jax.experimental.pallas import tpu as pltpu

F32 = jnp.float32
BF16 = jnp.bfloat16

D_MODEL = 2048
PAST_LEN = 1024
CHUNK = 64
RWKV_WIDTH = 1024
RWKV_HEAD_DIM = 64
RWKV_HEADS = 16
RWKV_DECAY_LORA = 64
RWKV_A_LORA = 64
RWKV_GATE_LORA = 160
RWKV_PROJ = 3 * RWKV_WIDTH + RWKV_DECAY_LORA + RWKV_A_LORA + RWKV_GATE_LORA
RWKV_PROJ_PAD = 3584
RWKV_GN_EPS = 64e-5
RET_HEAD_DIM = 128
RET_WIDTH = 1024
RET_HEADS = 8
RET_GN_EPS = 1e-5
ROPE_BASE = 10000.0
RET_CHUNK = 64
ATT_HEAD_DIM = 64
ATT_WIDTH = 1024
ATT_HEADS = 16
ATT_LEFT_CHUNKS = 8
ATT_PAST_ROWS = ATT_LEFT_CHUNKS * CHUNK
ATT_QBLK = 256
ATT_WIN = ATT_QBLK + ATT_PAST_ROWS
REL_CLIP = 128
OFF_RET = RWKV_PROJ_PAD
OFF_ATT = OFF_RET + 4 * RET_WIDTH
OFF_GATE = OFF_ATT + 3 * ATT_WIDTH
IN_PROJ_PAD = OFF_GATE + 3 * D_MODEL
N_EXPERTS = 64
TOP_K = 8
N_GROUPS = 8
TOPK_GROUPS = 4
EXPERT_DIM = 512
SHARED_DIM = 512
ROUTED_SCALE = 2.5
MOE_BLOCK = 512
LN_EPS = 1e-5
NEG_INF = -1e30
LANES = 128

VMEM_LIMIT = 56 * 1024 * 1024


def _cparams(sem):
    return pltpu.CompilerParams(dimension_semantics=sem, vmem_limit_bytes=VMEM_LIMIT)


def _tile(n, target, align=8):
    for t in range(min(n, target), 0, -1):
        if n % t == 0 and t % align == 0:
            return t
    return n


def _dot(a, b):
    return jnp.dot(a, b, preferred_element_type=F32)


def _dot_nt(a, b):
    return lax.dot_general(a, b, (((1,), (1,)), ((), ())), preferred_element_type=F32)


def _dot_tn(a, b):
    return lax.dot_general(a, b, (((0,), (0,)), ((), ())), preferred_element_type=F32)


def _sigmoid(x):
    return 1.0 / (1.0 + jnp.exp(-x))


def _split3(x):
    hi = x.astype(BF16)
    r1 = x - hi.astype(F32)
    mid = r1.astype(BF16)
    lo = (r1 - mid.astype(F32)).astype(BF16)
    return hi, mid, lo


def _mm_kernel(x_ref, w_ref, o_ref):
    o_ref[...] = _dot(x_ref[...], w_ref[...])


def _matmul(x, w, tm_target=1280, tn_target=768):
    m, k = x.shape
    n = w.shape[1]
    tm = _tile(m, tm_target)
    tn = _tile(n, tn_target, LANES)
    return pl.pallas_call(
        _mm_kernel,
        out_shape=jax.ShapeDtypeStruct((m, n), F32),
        grid=(m // tm, n // tn),
        in_specs=[pl.BlockSpec((tm, k), lambda i, j: (i, 0)),
                  pl.BlockSpec((k, tn), lambda i, j: (0, j))],
        out_specs=pl.BlockSpec((tm, tn), lambda i, j: (i, j)),
        compiler_params=_cparams(("parallel", "parallel")),
        name="in_proj",
    )(x, w)


def _rwkv_kernel(*refs, n_rows):
    p_refs = refs[:n_rows]
    (shift_ref, s0_ref, mu_ref, w0_ref, w2_ref, a0_ref, a2_ref, g2_ref, kk_ref, ka_ref, rk_ref,
     lng_ref, lnb_ref, e_ref, et_ref,
     y_ref, sout_ref, shout_ref,
     s_scr, sb_scr, prev_scr, w_scr, a_scr, r_scr, b_scr, k_scr, v_scr, yo_scr, g_scr) = refs[n_rows:]
    c = pl.program_id(1)
    n_c = pl.num_programs(1)
    C = p_refs[0].shape[0]
    n_pairs = RWKV_WIDTH // LANES
    hd = RWKV_HEAD_DIM
    ROW_W, ROW_R, ROW_B, ROW_K, ROW_V = range(5)

    @pl.when(c == 0)
    def _():
        s_scr[...] = s0_ref[...]
        sb_scr[...] = s0_ref[...].astype(BF16)
        prev_scr[...] = shift_ref[...]

    e_m = e_ref[...]
    et_m = et_ref[...]

    def headsum(x):
        s = None
        for limb in _split3(x):
            t = _dot(limb, e_m)
            s = t if s is None else s + t
        out = None
        for limb in _split3(s):
            t = _dot(limb, et_m)
            out = t if out is None else out + t
        return out

    w = RWKV_WIDTH
    for rr in range(n_rows):
        pf = p_refs[rr][...]
        row = lax.broadcasted_iota(jnp.int32, pf.shape, 0)
        prev = jnp.where(row == 0, prev_scr[rr, 0:1, :], pltpu.roll(pf, 1, 0))
        prev_scr[rr, 0:1, :] = pf[C - 1:C, :]
        px = pf + (prev - pf) * mu_ref[...]
        r = px[:, 0:w]
        k = px[:, w:2 * w]
        v = px[:, 2 * w:3 * w]
        lora = px[:, 3 * w:3 * w + LANES]
        xg = px[:, 3 * w + LANES:RWKV_PROJ_PAD]
        z = w0_ref[...] + _dot(jnp.tanh(lora).astype(BF16), w2_ref[...])
        w_log = -(jnp.maximum(-z, 0.0) + jnp.log1p(jnp.exp(-jnp.abs(z)))) - 0.5
        a = _sigmoid(a0_ref[...] + _dot(lora.astype(BF16), a2_ref[...]))
        g_scr[rr] = _dot(_sigmoid(xg).astype(BF16), g2_ref[...])
        kk = k * kk_ref[...]
        kk = kk / jnp.maximum(jnp.sqrt(headsum(kk * kk)), 1e-12)
        w_scr[rr] = jnp.exp(-jnp.exp(w_log))
        a_scr[rr] = -kk
        r_scr[rr] = r
        b_scr[rr] = kk * a
        k_scr[rr] = k * (1.0 + (a - 1.0) * ka_ref[...])
        v_scr[rr] = v
    yo_scr[...] = jnp.zeros(yo_scr.shape, F32)

    lane8 = lax.broadcasted_iota(jnp.int32, (8, LANES), 1)
    sub8 = lax.broadcasted_iota(jnp.int32, (8, LANES), 0)
    sub8h = sub8[:, 0:hd]
    hsel4 = jnp.logical_and((lane8 // hd) == (sub8 & 1), sub8 < 4)
    left_lo = (sub8 & 3) >= 2
    left_lo_h = left_lo[:, 0:hd]
    right_lo = (sub8 & 1) == 1
    right_head = (lane8 // hd) == (sub8 // 4)
    pairs = [(rr, p) for rr in range(n_rows) for p in range(n_pairs)]
    row_scrs = (w_scr, r_scr, b_scr, k_scr, v_scr)

    def limbs(x):
        hi = x.astype(BF16).astype(F32)
        return hi, x - hi

    def step(t, sas):
        t0 = pl.multiple_of((t // 8) * 8, 8)
        tj = t - t0
        t_nx = jnp.minimum(t + 1, C - 1)
        t0_nx = pl.multiple_of((t_nx // 8) * 8, 8)
        tj_nx = t_nx - t0_nx
        xs = []
        for rr, p in pairs:
            ls = slice(p * LANES, (p + 1) * LANES)
            x = None
            for q, scr in enumerate(row_scrs):
                tile = pltpu.roll(scr[rr, pl.ds(t0, 8), ls], (q + 8 - tj) & 7, 0)
                x = tile if x is None else jnp.where(sub8 == q, tile, x)
            xs.append(x)
        upds = []
        for x, sa2 in zip(xs, sas):
            x_hi, x_lo = limbs(x)
            s_hi, s_lo = limbs(sa2)
            l_sa = jnp.where(left_lo_h,
                             jnp.where(sub8h < 4, s_lo[2:3, :], s_lo[3:4, :]),
                             jnp.where(sub8h < 4, s_hi[2:3, :], s_hi[3:4, :]))
            v_sel = jnp.where(left_lo, x_lo[ROW_V:ROW_V + 1, :], x_hi[ROW_V:ROW_V + 1, :])
            l_v = jnp.where(sub8 < 4, v_sel, pltpu.roll(v_sel, hd, 1))[:, 0:hd]
            left = jnp.concatenate([l_sa, l_v], axis=0).astype(BF16)
            r_b = jnp.where(right_head, jnp.where(right_lo, x_lo[ROW_B:ROW_B + 1, :], x_hi[ROW_B:ROW_B + 1, :]), 0.0)
            r_k = jnp.where(right_head, jnp.where(right_lo, x_lo[ROW_K:ROW_K + 1, :], x_hi[ROW_K:ROW_K + 1, :]), 0.0)
            right = jnp.concatenate([r_b, r_k], axis=0).astype(BF16)
            upds.append(_dot_tn(left, right))
        sbs = []
        for (rr, p), x, upd in zip(pairs, xs, upds):
            s_new = s_scr[rr, p] * x[ROW_W:ROW_W + 1, :] + upd
            s_scr[rr, p] = s_new
            sb = s_new.astype(BF16)
            sb_scr[rr, p] = sb
            sbs.append(sb)
        sas_nx = []
        for (rr, p), x, sb in zip(pairs, xs, sbs):
            ls = slice(p * LANES, (p + 1) * LANES)
            a_nx = pltpu.roll(a_scr[rr, pl.ds(t0_nx, 8), ls], (8 - tj_nx) & 7, 0)[0:1, :]
            lhs = jnp.where(hsel4, jnp.where(sub8 < 2, x[ROW_R:ROW_R + 1, :], a_nx), 0.0).astype(BF16)
            out = _dot_nt(lhs, sb)
            y_row = jnp.concatenate([out[0:1, :], out[1:2, :]], axis=1)
            yo_scr[rr, pl.ds(t0, 8), ls] = jnp.where(sub8 == tj, y_row, yo_scr[rr, pl.ds(t0, 8), ls])
            sas_nx.append(out)
        return tuple(sas_nx)

    sa_init = []
    for rr, p in pairs:
        ls = slice(p * LANES, (p + 1) * LANES)
        lhs = jnp.where(hsel4, a_scr[rr, 0:1, ls], 0.0).astype(BF16)
        sa_init.append(_dot_nt(lhs, sb_scr[rr, p]))
    lax.fori_loop(0, C, step, tuple(sa_init))

    inv_n = 1.0 / hd
    for rr in range(n_rows):
        y = yo_scr[rr]
        mean = headsum(y) * inv_n
        d = y - mean
        var = headsum(d * d) * inv_n
        yn = d * lax.rsqrt(var + RWKV_GN_EPS) * lng_ref[...] + lnb_ref[...]
        bonus = headsum(r_scr[rr] * k_scr[rr] * rk_ref[...]) * v_scr[rr]
        y_ref[rr] = ((yn + bonus) * g_scr[rr]).astype(y_ref.dtype)

    @pl.when(c == n_c - 1)
    def _():
        sout_ref[...] = s_scr[...]
        shout_ref[...] = prev_scr[...]


def _rwkv_consts():
    lane = np.arange(RWKV_WIDTH)
    e = (lane[:, None] // RWKV_HEAD_DIM == np.arange(LANES)[None, :]).astype(np.float32)
    return jnp.asarray(e, BF16), jnp.asarray(e.T, BF16)


def _rwkv_call(h, row_blk0, n_b, n_t, shift0, s0_pairs, lp):
    C = CHUNK
    n_c = n_t // C
    n_rows = 4 if n_b % 4 == 0 else (2 if n_b % 2 == 0 else 1)
    n_pairs = RWKV_WIDTH // LANES
    consts = _rwkv_consts()
    full = lambda arr: pl.BlockSpec(arr.shape, lambda i, c: (0,) * arr.ndim)
    params = [lp['rwkv_mu'], lp['rwkv_w0'], lp['rwkv_w2'], lp['rwkv_a0'], lp['rwkv_a2'], lp['rwkv_g2'],
              lp['rwkv_k_k'], lp['rwkv_k_a'], lp['rwkv_r_k'], lp['rwkv_ln_g'], lp['rwkv_ln_b']]

    def pspec(rr):
        return pl.BlockSpec((C, RWKV_PROJ_PAD), lambda i, c: (row_blk0 + (i * n_rows + rr) * n_c + c, 0))
    state_spec = pl.BlockSpec((n_rows, n_pairs, RWKV_HEAD_DIM, LANES), lambda i, c: (i, 0, 0, 0))
    shift_spec = pl.BlockSpec((n_rows, 8, RWKV_PROJ_PAD), lambda i, c: (i, 0, 0))
    rows_scr = pltpu.VMEM((n_rows, C, RWKV_WIDTH), F32)
    y, s_out, sh_out = pl.pallas_call(
        functools.partial(_rwkv_kernel, n_rows=n_rows),
        out_shape=(jax.ShapeDtypeStruct((n_b, n_t, RWKV_WIDTH), BF16),
                   jax.ShapeDtypeStruct((n_b, n_pairs, RWKV_HEAD_DIM, LANES), F32),
                   jax.ShapeDtypeStruct((n_b, 8, RWKV_PROJ_PAD), F32)),
        grid=(n_b // n_rows, n_c),
        in_specs=[pspec(rr) for rr in range(n_rows)] + [shift_spec, state_spec]
                 + [full(a) for a in params] + [full(a) for a in consts],
        out_specs=(pl.BlockSpec((n_rows, C, RWKV_WIDTH), lambda i, c: (i, c, 0)), state_spec, shift_spec),
        scratch_shapes=[pltpu.VMEM((n_rows, n_pairs, RWKV_HEAD_DIM, LANES), F32),
                        pltpu.VMEM((n_rows, n_pairs, RWKV_HEAD_DIM, LANES), BF16),
                        pltpu.VMEM((n_rows, 8, RWKV_PROJ_PAD), F32)] + [rows_scr] * 8,
        compiler_params=_cparams(("parallel", "arbitrary")),
        name="rwkv7_scan",
    )(*([h] * n_rows), shift0, s0_pairs, *params, *consts)
    return y.reshape(n_b * n_t, RWKV_WIDTH), s_out, sh_out


def _rwkv_state_to_pairs(s):
    n_b = s.shape[0]
    s5 = s.astype(F32).reshape(n_b, RWKV_HEADS // 2, 2, RWKV_HEAD_DIM, RWKV_HEAD_DIM)
    return s5.transpose(0, 1, 3, 2, 4).reshape(n_b, RWKV_HEADS // 2, RWKV_HEAD_DIM, LANES)


def _rwkv_state_from_pairs(sp):
    n_b = sp.shape[0]
    s5 = sp.reshape(n_b, RWKV_HEADS // 2, RWKV_HEAD_DIM, 2, RWKV_HEAD_DIM)
    return s5.transpose(0, 1, 3, 2, 4).reshape(n_b, RWKV_HEADS, RWKV_HEAD_DIM, RWKV_HEAD_DIM)


def _ret_kernel(*refs, chunk_decay):
    (q0, q1, k0, k1, v0, v1, g0, g1, cc_ref, ss_ref, dec_ref, qd_ref, kd_ref, s0_ref,
     lng_ref, lnb_ref, y_ref, sout_ref, s_scr) = refs
    c = pl.program_id(1)
    n_c = pl.num_programs(1)
    n_h = s_scr.shape[0]
    hp = n_h // 2
    d = RET_HEAD_DIM

    @pl.when(c == 0)
    def _():
        s_scr[...] = s0_ref[0]

    cc = cc_ref[...]
    ss = ss_ref[...]
    outs = []
    for j in range(n_h):
        q_ref, k_ref, v_ref = ((q0, k0, v0), (q1, k1, v1))[j // hp]
        sl = slice((j % hp) * d, (j % hp + 1) * d)
        q = q_ref[:, sl]
        k = k_ref[:, sl]
        v = v_ref[:, sl].astype(BF16)
        qr = q * cc + pltpu.roll(q, d // 2, 1) * ss
        kr = (k * cc + pltpu.roll(k, d // 2, 1) * ss) * (d ** -0.5)
        qb = qr.astype(BF16)
        scores = _dot_nt(qb, kr.astype(BF16)) * dec_ref[j]
        inner = _dot(scores.astype(BF16), v)
        s_j = s_scr[j]
        cross = _dot((qr * qd_ref[j]).astype(BF16), s_j.astype(BF16))
        kv = _dot_tn((kr * kd_ref[j]).astype(BF16), v)
        s_scr[j] = s_j * chunk_decay[j] + kv
        y = inner + cross
        mu = jnp.mean(y, axis=-1, keepdims=True)
        yc = y - mu
        var = jnp.mean(yc * yc, axis=-1, keepdims=True)
        outs.append(yc * lax.rsqrt(var + RET_GN_EPS))
    yn = jnp.concatenate(outs, axis=1) * lng_ref[...] + lnb_ref[...]
    gt = jnp.concatenate([g0[...], g1[...]], axis=1)
    y_ref[...] = (gt * _sigmoid(gt) * yn).astype(y_ref.dtype)

    @pl.when(c == n_c - 1)
    def _():
        sout_ref[0] = s_scr[...]


def _ret_consts(C, pos0, n_t):
    f = np.float32
    log_gamma = np.log1p(-np.exp2(-5.0 - np.arange(RET_HEADS, dtype=f))).astype(f)
    idx = np.arange(C, dtype=f)
    diff = idx[:, None] - idx[None, :]
    dec = np.where(diff >= 0, np.exp(log_gamma[:, None, None] * np.maximum(diff, 0.0)), 0.0).astype(f)
    qd = np.exp(log_gamma[:, None] * (idx + 1.0)[None, :]).astype(f)
    kd = np.exp(log_gamma[:, None] * (C - 1.0 - idx)[None, :]).astype(f)
    qd = np.broadcast_to(qd[:, :, None], (RET_HEADS, C, RET_HEAD_DIM)).copy()
    kd = np.broadcast_to(kd[:, :, None], (RET_HEADS, C, RET_HEAD_DIM)).copy()
    cd = tuple(float(x) for x in np.exp(log_gamma * C))
    half = RET_HEAD_DIM // 2
    inv = (ROPE_BASE ** (-np.arange(half, dtype=f) / half)).astype(f)
    pos = (pos0 + np.arange(n_t)).astype(f)
    ang = pos[:, None] * inv[None, :]
    cos, sin = np.cos(ang).astype(f), np.sin(ang).astype(f)
    cc = np.concatenate([cos, cos], axis=1)
    ss = np.concatenate([-sin, sin], axis=1)
    return jnp.asarray(dec), jnp.asarray(qd), jnp.asarray(kd), cd, jnp.asarray(cc), jnp.asarray(ss)


def _ret_call(h, row_blk0, n_b, n_t, pos0, s0, lp):
    C = min(RET_CHUNK, n_t)
    n_c = n_t // C
    hw = 512
    dec, qd, kd, cd, cc, ss = _ret_consts(C, pos0, n_t)
    col0 = OFF_RET // hw
    blk = lambda col: pl.BlockSpec((C, hw), lambda b, c: (row_blk0 + b * n_c + c, col0 + col))
    full = lambda arr: pl.BlockSpec(arr.shape, lambda b, c: (0,) * arr.ndim)
    state = pl.BlockSpec((1, RET_HEADS, RET_HEAD_DIM, RET_HEAD_DIM), lambda b, c: (b, 0, 0, 0))
    y, s_out = pl.pallas_call(
        functools.partial(_ret_kernel, chunk_decay=cd),
        out_shape=(jax.ShapeDtypeStruct((n_b * n_t, RET_WIDTH), BF16),
                   jax.ShapeDtypeStruct((n_b, RET_HEADS, RET_HEAD_DIM, RET_HEAD_DIM), F32)),
        grid=(n_b, n_c),
        in_specs=[blk(col) for col in range(8)]
                 + [pl.BlockSpec((C, RET_HEAD_DIM), lambda b, c: (c, 0)),
                    pl.BlockSpec((C, RET_HEAD_DIM), lambda b, c: (c, 0)),
                    full(dec), full(qd), full(kd), state, full(lp['ret_ln_g']), full(lp['ret_ln_b'])],
        out_specs=(pl.BlockSpec((C, RET_WIDTH), lambda b, c: (b * n_c + c, 0)), state),
        scratch_shapes=[pltpu.VMEM((RET_HEADS, RET_HEAD_DIM, RET_HEAD_DIM), F32)],
        compiler_params=_cparams(("parallel", "arbitrary")),
        name="retention_chunk",
    )(*([h] * 8), cc, ss, dec, qd, kd, s0.astype(F32), lp['ret_ln_g'], lp['ret_ln_b'])
    return y, s_out


def _att_kernel(q_ref, k0_ref, k1_ref, k2_ref, v0_ref, v1_ref, v2_ref, bias_ref, o_ref, *, blk0):
    blk = pl.program_id(2) + blk0
    hd = ATT_HEAD_DIM
    n_pairs = q_ref.shape[1] // LANES
    lane = lax.broadcasted_iota(jnp.int32, (1, LANES), 1)
    k_refs = (k0_ref, k1_ref, k2_ref)
    v_refs = (v0_ref, v1_ref, v2_ref)
    valid = (blk >= 2, blk >= 1, None)
    outs = []
    for p in range(n_pairs):
        sl = slice(p * LANES, (p + 1) * LANES)
        q = q_ref[:, sl]
        ks = [kr[:, sl].astype(BF16) for kr in k_refs]
        vs = [vr[:, sl].astype(BF16) for vr in v_refs]
        o_heads = []
        for j in range(2):
            qm = jnp.where((lane // hd) == j, q, 0.0).astype(BF16)
            parts = []
            for kb in range(3):
                s = _dot_nt(qm, ks[kb]) * (hd ** -0.5)
                s = s + bias_ref[2 * p + j, :, kb * ATT_QBLK:(kb + 1) * ATT_QBLK]
                if valid[kb] is not None:
                    s = jnp.where(valid[kb], s, NEG_INF)
                parts.append(s)
            s = jnp.concatenate(parts, axis=1)
            m = jnp.max(s, axis=-1, keepdims=True)
            e = jnp.exp(s - m)
            pr = (e / jnp.sum(e, axis=-1, keepdims=True)).astype(BF16)
            o = _dot(pr[:, 0:ATT_QBLK], vs[0])
            o = o + _dot(pr[:, ATT_QBLK:2 * ATT_QBLK], vs[1])
            o = o + _dot(pr[:, 2 * ATT_QBLK:3 * ATT_QBLK], vs[2])
            o_heads.append(o)
        outs.append(jnp.where((lane // hd) == 0, o_heads[0], o_heads[1]))
    o_ref[...] = jnp.concatenate(outs, axis=1).astype(o_ref.dtype)


def _att_bias(table):
    n_h = table.shape[0]
    r = np.arange(ATT_QBLK)[:, None]
    w = np.arange(ATT_WIN)[None, :]
    lo = CHUNK * (r // CHUNK)
    band = (w >= lo) & (w < lo + ATT_PAST_ROWS + CHUNK)
    period = ATT_QBLK + ATT_WIN + 1
    j = np.arange(period)
    d_idx = np.clip(ATT_QBLK - j + ATT_PAST_ROWS, -REL_CLIP, REL_CLIP) + REL_CLIP
    d = table.astype(F32)[:, d_idx]
    skew = jnp.tile(d, (1, ATT_QBLK))[:, :ATT_QBLK * (period - 1)].reshape(n_h, ATT_QBLK, period - 1)
    b = skew[:, :, ATT_QBLK:ATT_QBLK + ATT_WIN]
    return jnp.where(jnp.asarray(band)[None], b, NEG_INF)


def _att_call(q_arr, k_arr, v_arr, cols, row_blk0, n_b, n_blk_batch, blk0, n_blk, bias):
    gw = 256
    n_hg = ATT_WIDTH // gw
    hpg = gw // ATT_HEAD_DIM
    qc, kc, vc = cols
    rowb = lambda b, i: row_blk0 + b * n_blk_batch + i
    qspec = pl.BlockSpec((ATT_QBLK, gw), lambda g, b, i: (rowb(b, i + blk0), qc + g))

    def kvspec(col, back):
        return pl.BlockSpec((ATT_QBLK, gw),
                            lambda g, b, i: (rowb(b, jnp.maximum(i + blk0 - back, 0)), col + g))
    return pl.pallas_call(
        functools.partial(_att_kernel, blk0=blk0),
        out_shape=jax.ShapeDtypeStruct((n_b * n_blk * ATT_QBLK, ATT_WIDTH), BF16),
        grid=(n_hg, n_b, n_blk),
        in_specs=[qspec, kvspec(kc, 2), kvspec(kc, 1), kvspec(kc, 0),
                  kvspec(vc, 2), kvspec(vc, 1), kvspec(vc, 0),
                  pl.BlockSpec((hpg, ATT_QBLK, ATT_WIN), lambda g, b, i: (g, 0, 0))],
        out_specs=pl.BlockSpec((ATT_QBLK, gw), lambda g, b, i: (b * n_blk + i, g)),
        compiler_params=_cparams(("parallel", "parallel", "arbitrary")),
        name="band_attention",
    )(q_arr, k_arr, k_arr, k_arr, v_arr, v_arr, v_arr, bias)


def _merge_kernel(ya_ref, yr_ref, yc_ref, ga_ref, gr_ref, gc_ref, wa_ref, wr_ref, wc_ref, o_ref):
    m = _sigmoid(ga_ref[...]) * _dot(ya_ref[...], wa_ref[...])
    m = m + _sigmoid(gr_ref[...]) * _dot(yr_ref[...], wr_ref[...])
    m = m + _sigmoid(gc_ref[...]) * _dot(yc_ref[...], wc_ref[...])
    o_ref[...] = m.astype(o_ref.dtype)


def _merge_call(ya, yr, yc, h, lp):
    m = ya.shape[0]
    tm = _tile(m, 512)
    tn = 512
    n_n = D_MODEL // tn
    g0 = OFF_GATE // tn
    yspec = pl.BlockSpec((tm, RWKV_WIDTH), lambda i, j: (i, 0))
    wspec = pl.BlockSpec((RWKV_WIDTH, tn), lambda i, j: (0, j))
    gspec = lambda part: pl.BlockSpec((tm, tn), lambda i, j: (i, g0 + part * n_n + j))
    return pl.pallas_call(
        _merge_kernel,
        out_shape=jax.ShapeDtypeStruct((m, D_MODEL), BF16),
        grid=(m // tm, n_n),
        in_specs=[yspec, yspec, yspec, gspec(0), gspec(1), gspec(2), wspec, wspec, wspec],
        out_specs=pl.BlockSpec((tm, tn), lambda i, j: (i, j)),
        compiler_params=_cparams(("parallel", "parallel")),
        name="branch_merge",
    )(ya, yr, yc, h, h, h, lp['w_branch_rwkv'], lp['w_branch_ret'], lp['w_branch_att'])


def _layer_norm(z, g, b):
    mu = jnp.mean(z, axis=-1, keepdims=True)
    zc = z - mu
    var = jnp.mean(zc * zc, axis=-1, keepdims=True)
    return zc * lax.rsqrt(var + LN_EPS) * g + b


def _outproj_kernel(x_ref, m_ref, w_ref, g_ref, b_ref, o_ref, ob_ref, *, alpha):
    z = alpha * x_ref[...] + _dot(m_ref[...], w_ref[...])
    y = _layer_norm(z, g_ref[...], b_ref[...])
    o_ref[...] = y
    ob_ref[...] = y.astype(ob_ref.dtype)


def _outproj_call(x, merged, lp, alpha):
    m = x.shape[0]
    tm = _tile(m, 256)
    row = pl.BlockSpec((tm, D_MODEL), lambda i: (i, 0))
    vec = pl.BlockSpec((1, D_MODEL), lambda i: (0, 0))
    return pl.pallas_call(
        functools.partial(_outproj_kernel, alpha=alpha),
        out_shape=(jax.ShapeDtypeStruct((m, D_MODEL), F32), jax.ShapeDtypeStruct((m, D_MODEL), BF16)),
        grid=(m // tm,),
        in_specs=[row, row, pl.BlockSpec((D_MODEL, D_MODEL), lambda i: (0, 0)), vec, vec],
        out_specs=(row, row),
        compiler_params=_cparams(("parallel",)),
        name="out_proj_ln1",
    )(x, merged, lp['w_out'], lp['ln1_g'], lp['ln1_b'])


def _router_kernel(x_ref, w_ref, b_ref, idx_ref, gate_ref):
    x = x_ref[...]
    logits = _dot(x, w_ref[...])
    scores = _sigmoid(logits)
    tm = x.shape[0]
    lane_i = lax.broadcasted_iota(jnp.int32, (tm, LANES), 1)
    lane = lane_i.astype(F32)
    real = lane_i < N_EXPERTS
    sel = jnp.where(real, scores + b_ref[...], NEG_INF)
    per_group = N_EXPERTS // N_GROUPS
    grp = (lane_i // per_group).astype(F32)

    def first_argmax(vals):
        m = jnp.max(vals, axis=-1, keepdims=True)
        i = jnp.min(jnp.where(vals == m, lane, float(LANES)), axis=-1, keepdims=True)
        return m, i

    gscore = jnp.full((tm, LANES), NEG_INF, F32)
    for gidx in range(N_GROUPS):
        in_g = grp == gidx
        vals = jnp.where(in_g, sel, -jnp.inf)
        m1, i1 = first_argmax(vals)
        m2 = jnp.max(jnp.where(lane == i1, -jnp.inf, vals), axis=-1, keepdims=True)
        gscore = jnp.where(in_g, m1 + m2, gscore)
    chosen = jnp.zeros((tm, LANES), jnp.bool_)
    cand = jnp.where(real, gscore, -jnp.inf)
    for _ in range(TOPK_GROUPS):
        _, i = first_argmax(cand)
        pick = grp == jnp.floor(i * (1.0 / per_group))
        chosen = jnp.logical_or(chosen, pick)
        cand = jnp.where(pick, -jnp.inf, cand)
    cand = jnp.where(real, jnp.where(chosen, sel, NEG_INF), -jnp.inf)
    idx_out = jnp.zeros((tm, LANES), F32)
    w_out = jnp.zeros((tm, LANES), F32)
    for kk in range(TOP_K):
        _, i = first_argmax(cand)
        hit = lane == i
        wk = jnp.sum(jnp.where(hit, scores, 0.0), axis=-1, keepdims=True)
        idx_out = jnp.where(lane == kk, i, idx_out)
        w_out = jnp.where(lane == kk, wk, w_out)
        cand = jnp.where(hit, -jnp.inf, cand)
    total = jnp.sum(w_out, axis=-1, keepdims=True)
    idx_ref[...] = idx_out.astype(jnp.int32)
    gate_ref[...] = w_out / total * ROUTED_SCALE


def _router_call(x, lp):
    m = x.shape[0]
    tm = _tile(m, 512)
    row = pl.BlockSpec((tm, LANES), lambda i: (i, 0))
    return pl.pallas_call(
        _router_kernel,
        out_shape=(jax.ShapeDtypeStruct((m, LANES), jnp.int32), jax.ShapeDtypeStruct((m, LANES), F32)),
        grid=(m // tm,),
        in_specs=[pl.BlockSpec((tm, D_MODEL), lambda i: (i, 0)),
                  pl.BlockSpec((D_MODEL, LANES), lambda i: (0, 0)),
                  pl.BlockSpec((1, LANES), lambda i: (0, 0))],
        out_specs=(row, row),
        compiler_params=_cparams(("parallel",)),
        name="router_topk",
    )(x, lp['router_w'], lp['router_bias'])


def _expert_kernel(be_ref, nu_ref, x_ref, wgu_ref, wdn_ref, o_ref, wgu_scr, wdn_scr):
    i = pl.program_id(0)

    @pl.when(jnp.logical_or(i == 0, be_ref[i] != be_ref[jnp.maximum(i - 1, 0)]))
    def _():
        wgu_scr[...] = wgu_ref[0, 0].astype(BF16)
        wdn_scr[...] = wdn_ref[0, 0].astype(BF16)

    @pl.when(i < nu_ref[0])
    def _():
        hgu = _dot(x_ref[...], wgu_scr[...])
        gt = hgu[:, :EXPERT_DIM]
        up = hgu[:, EXPERT_DIM:]
        act = (gt * _sigmoid(gt) * up).astype(BF16)
        o_ref[...] = _dot(act, wdn_scr[...]).astype(o_ref.dtype)

    @pl.when(i >= nu_ref[0])
    def _():
        o_ref[...] = jnp.zeros_like(o_ref)


def _expert_call(xs, block_expert, n_used, lp):
    n_slots = xs.shape[0]
    bm = MOE_BLOCK
    n_blocks = n_slots // bm
    layer = lp['layer']
    grid_spec = pltpu.PrefetchScalarGridSpec(
        num_scalar_prefetch=2,
        grid=(n_blocks,),
        in_specs=[pl.BlockSpec((bm, D_MODEL), lambda i, be, nu: (i, 0)),
                  pl.BlockSpec((1, 1, D_MODEL, 2 * EXPERT_DIM), lambda i, be, nu: (layer, be[i], 0, 0)),
                  pl.BlockSpec((1, 1, EXPERT_DIM, D_MODEL), lambda i, be, nu: (layer, be[i], 0, 0))],
        out_specs=pl.BlockSpec((bm, D_MODEL), lambda i, be, nu: (i, 0)),
        scratch_shapes=[pltpu.VMEM((D_MODEL, 2 * EXPERT_DIM), BF16), pltpu.VMEM((EXPERT_DIM, D_MODEL), BF16)],
    )
    return pl.pallas_call(
        _expert_kernel,
        out_shape=jax.ShapeDtypeStruct((n_slots, D_MODEL), BF16),
        grid_spec=grid_spec,
        compiler_params=_cparams(("arbitrary",)),
        name="routed_experts",
    )(block_expert, n_used, xs, lp['expert_w_gate_up'], lp['expert_w_down'])


def _combine_kernel(x_ref, xb_ref, yg_ref, gate_ref, wgu_ref, wdn_ref, g_ref, b_ref, o_ref, ob_ref, *, alpha):
    x = x_ref[...]
    hgu = _dot(xb_ref[...], wgu_ref[...])
    gt = hgu[:, :SHARED_DIM]
    up = hgu[:, SHARED_DIM:]
    moe = _dot((gt * _sigmoid(gt) * up).astype(BF16), wdn_ref[...])
    gate = gate_ref[...].astype(BF16).astype(F32)
    for kk in range(TOP_K):
        moe = moe + gate[:, kk:kk + 1] * yg_ref[kk].astype(F32)
    y = _layer_norm(alpha * x + moe, g_ref[...], b_ref[...])
    o_ref[...] = y
    ob_ref[...] = y.astype(ob_ref.dtype)


def _combine_call(x, xb, yg, gate, lp, alpha):
    m = x.shape[0]
    tm = _tile(m, 128)
    row = pl.BlockSpec((tm, D_MODEL), lambda i: (i, 0))
    vec = pl.BlockSpec((1, D_MODEL), lambda i: (0, 0))
    return pl.pallas_call(
        functools.partial(_combine_kernel, alpha=alpha),
        out_shape=(jax.ShapeDtypeStruct((m, D_MODEL), F32), jax.ShapeDtypeStruct((m, D_MODEL), BF16)),
        grid=(m // tm,),
        in_specs=[row, row,
                  pl.BlockSpec((TOP_K, tm, D_MODEL), lambda i: (0, i, 0)),
                  pl.BlockSpec((tm, LANES), lambda i: (i, 0)),
                  pl.BlockSpec((D_MODEL, 2 * SHARED_DIM), lambda i: (0, 0)),
                  pl.BlockSpec((SHARED_DIM, D_MODEL), lambda i: (0, 0)), vec, vec],
        out_specs=(row, row),
        compiler_params=_cparams(("parallel",)),
        name="moe_combine_ln2",
    )(x, xb, yg, gate, lp['shared_w_gate_up'], lp['shared_w_down'], lp['ln2_g'], lp['ln2_b'])


def _dispatch_plan(idx):
    n_tok = idx.shape[0]
    n_a = n_tok * TOP_K
    bm = MOE_BLOCK
    n_blocks = n_a // bm + N_EXPERTS
    e_flat = idx.reshape(n_a)
    order = jnp.argsort(e_flat, stable=True).astype(jnp.int32)
    rank = jnp.argsort(order).astype(jnp.int32)
    counts = jnp.sum(e_flat[:, None] == jnp.arange(N_EXPERTS, dtype=jnp.int32)[None, :], axis=0,
                     dtype=jnp.int32)
    padded = (counts + bm - 1) // bm * bm
    pad_end = jnp.cumsum(padded)
    pad_start = pad_end - padded
    start = jnp.cumsum(counts) - counts
    slot_of_assign = pad_start[e_flat] + rank - start[e_flat]
    blk_start = jnp.arange(n_blocks, dtype=jnp.int32) * bm
    block_expert = jnp.minimum(jnp.sum(pad_end[None, :] <= blk_start[:, None], axis=1, dtype=jnp.int32),
                               N_EXPERTS - 1)
    n_used = (pad_end[-1] // bm).astype(jnp.int32).reshape(1)
    last_e = block_expert[jnp.maximum(n_used[0] - 1, 0)]
    block_expert = jnp.where(jnp.arange(n_blocks) < n_used[0], block_expert, last_e)
    slot_e = jnp.repeat(block_expert, bm)
    j = jnp.arange(n_blocks * bm, dtype=jnp.int32) - pad_start[slot_e]
    src = order[jnp.clip(start[slot_e] + j, 0, n_a - 1)] // TOP_K
    slot_tok = jnp.where(j < counts[slot_e], src, 0)
    return slot_tok, slot_of_assign, block_expert, n_used


def _moe_call(x1, x1b, lp, alpha):
    idx_p, gate_p = _router_call(x1b, lp)
    idx = idx_p[:, :TOP_K]
    slot_tok, slot_of_assign, block_expert, n_used = _dispatch_plan(idx)
    n_tok = x1.shape[0]
    xs = x1b.at[slot_tok].get(mode='promise_in_bounds')
    ys = _expert_call(xs, block_expert, n_used, lp)
    slot_kmajor = slot_of_assign.reshape(n_tok, TOP_K).T.reshape(-1)
    yg = ys.at[slot_kmajor].get(mode='promise_in_bounds').reshape(TOP_K, n_tok, D_MODEL)
    return _combine_call(x1, x1b, yg, gate_p, lp, alpha)


def _prep_layer(l, w_in, rwkv_mu, rwkv_w0, rwkv_w2, rwkv_a0, rwkv_a2, rwkv_g2, rwkv_k_k, rwkv_k_a,
                rwkv_r_k, rwkv_ln_g, rwkv_ln_b, ret_ln_g, ret_ln_b, att_rel_bias, w_branch_rwkv,
                w_branch_ret, w_branch_att, w_out, ln1_g, ln1_b, router_w, router_bias,
                expert_w_gate_up, expert_w_down, shared_w_gate_up, shared_w_down, ln2_g, ln2_b):
    pad_c = RWKV_PROJ_PAD - RWKV_PROJ
    wi = w_in[l]
    wi = jnp.concatenate([wi[:, :RWKV_PROJ], jnp.zeros((D_MODEL, pad_c), wi.dtype), wi[:, RWKV_PROJ:]], axis=1)
    rowv = lambda a: a.reshape(1, -1).astype(F32)
    zrows = lambda n: jnp.zeros((n, RWKV_WIDTH), F32)
    g_rows = RWKV_PROJ_PAD - 3 * RWKV_WIDTH - LANES
    return {
        'w_in': wi.astype(BF16),
        'rwkv_mu': jnp.pad(rowv(rwkv_mu[l]), ((0, 0), (0, pad_c))),
        'rwkv_w0': rowv(rwkv_w0[l]),
        'rwkv_w2': jnp.concatenate([rwkv_w2[l], zrows(RWKV_A_LORA)], 0).astype(BF16),
        'rwkv_a0': rowv(rwkv_a0[l]),
        'rwkv_a2': jnp.concatenate([zrows(RWKV_DECAY_LORA), rwkv_a2[l]], 0).astype(BF16),
        'rwkv_g2': jnp.concatenate([rwkv_g2[l], zrows(g_rows - RWKV_GATE_LORA)], 0).astype(BF16),
        'rwkv_k_k': rowv(rwkv_k_k[l]), 'rwkv_k_a': rowv(rwkv_k_a[l]), 'rwkv_r_k': rowv(rwkv_r_k[l]),
        'rwkv_ln_g': rowv(rwkv_ln_g[l]), 'rwkv_ln_b': rowv(rwkv_ln_b[l]),
        'ret_ln_g': rowv(ret_ln_g[l]), 'ret_ln_b': rowv(ret_ln_b[l]),
        'att_bias': _att_bias(att_rel_bias[l]),
        'w_branch_rwkv': w_branch_rwkv[l].astype(BF16), 'w_branch_ret': w_branch_ret[l].astype(BF16),
        'w_branch_att': w_branch_att[l].astype(BF16), 'w_out': w_out[l].astype(BF16),
        'ln1_g': rowv(ln1_g[l]), 'ln1_b': rowv(ln1_b[l]),
        'router_w': jnp.pad(router_w[l], ((0, 0), (0, LANES - N_EXPERTS))).astype(BF16),
        'router_bias': jnp.pad(rowv(router_bias[l]), ((0, 0), (0, LANES - N_EXPERTS))),
        'layer': l, 'expert_w_gate_up': expert_w_gate_up, 'expert_w_down': expert_w_down,
        'shared_w_gate_up': shared_w_gate_up[l].astype(BF16), 'shared_w_down': shared_w_down[l].astype(BF16),
        'ln2_g': rowv(ln2_g[l]), 'ln2_b': rowv(ln2_b[l]),
    }


def _pad_shift(shift):
    s = jnp.pad(shift.astype(F32), ((0, 0), (0, RWKV_PROJ_PAD - RWKV_PROJ)))
    return jnp.pad(s[:, None, :], ((0, 0), (0, 7), (0, 0)))


def kernel(x_prompt, x_sample, cache_attn_k, cache_attn_v, state_rwkv, state_rwkv_shift, state_ret, w_in, rwkv_mu, rwkv_w0, rwkv_w2, rwkv_a0, rwkv_a2, rwkv_g2, rwkv_k_k, rwkv_k_a, rwkv_r_k, rwkv_ln_g, rwkv_ln_b, ret_ln_g, ret_ln_b, att_rel_bias, w_branch_rwkv, w_branch_ret, w_branch_att, w_out, ln1_g, ln1_b, router_w, router_bias, expert_w_gate_up, expert_w_down, shared_w_gate_up, shared_w_down, ln2_g, ln2_b):
    n_bp, n_s, _ = x_prompt.shape
    n_bs, n_t, _ = x_sample.shape
    depth = w_in.shape[0]
    n_p = n_bp * n_s
    n_q = n_bs * n_t
    assert n_s % ATT_QBLK == 0 and n_t == CHUNK and n_p % ATT_QBLK == 0
    alpha = float((2 * depth) ** 0.25)
    weights = (w_in, rwkv_mu, rwkv_w0, rwkv_w2, rwkv_a0, rwkv_a2, rwkv_g2, rwkv_k_k, rwkv_k_a, rwkv_r_k,
               rwkv_ln_g, rwkv_ln_b, ret_ln_g, ret_ln_b, att_rel_bias, w_branch_rwkv, w_branch_ret,
               w_branch_att, w_out, ln1_g, ln1_b, router_w, router_bias, expert_w_gate_up, expert_w_down,
               shared_w_gate_up, shared_w_down, ln2_g, ln2_b)

    x = jnp.concatenate([x_prompt.reshape(n_p, D_MODEL), x_sample.reshape(n_q, D_MODEL)], axis=0)
    xb = x.astype(BF16)
    l_c = cache_attn_k.shape[2]
    assert l_c == ATT_PAST_ROWS
    new_p = [[], [], [], [], []]
    new_s = [[], [], [], [], []]
    zeros_shift = jnp.zeros((n_bp, 8, RWKV_PROJ_PAD), F32)
    zeros_rwkv = jnp.zeros((n_bp, RWKV_HEADS // 2, RWKV_HEAD_DIM, LANES), F32)
    zeros_ret = jnp.zeros((n_bp, RET_HEADS, RET_HEAD_DIM, RET_HEAD_DIM), F32)
    for l in range(depth):
        lp = _prep_layer(l, *weights)
        h = _matmul(xb, lp['w_in'])
        hq = h[n_p:]

        ya_p, rs_p, sh_p = _rwkv_call(h, 0, n_bp, n_s, zeros_shift, zeros_rwkv, lp)
        ya_s, rs_s, sh_s = _rwkv_call(h, n_p // CHUNK, n_bs, n_t, _pad_shift(state_rwkv_shift[l]),
                                      _rwkv_state_to_pairs(state_rwkv[l]), lp)
        yr_p, ts_p = _ret_call(h, 0, n_bp, n_s, 0, zeros_ret, lp)
        yr_s, ts_s = _ret_call(h, n_p // n_t, n_bs, n_t, PAST_LEN, state_ret[l], lp)
        cq = OFF_ATT // 256
        cols = (cq, cq + ATT_WIDTH // 256, cq + 2 * ATT_WIDTH // 256)
        yc_p = _att_call(h, h, h, cols, 0, n_bp, n_s // ATT_QBLK, 0, n_s // ATT_QBLK, lp['att_bias'])
        q_s = hq[:, OFF_ATT:OFF_ATT + ATT_WIDTH].reshape(n_bs, n_t, ATT_WIDTH)
        k_s = hq[:, OFF_ATT + ATT_WIDTH:OFF_ATT + 2 * ATT_WIDTH].reshape(n_bs, n_t, ATT_WIDTH)
        v_s = hq[:, OFF_ATT + 2 * ATT_WIDTH:OFF_ATT + 3 * ATT_WIDTH].reshape(n_bs, n_t, ATT_WIDTH)
        win_rows = 3 * ATT_QBLK
        lead = ATT_PAST_ROWS - l_c
        tail = win_rows - ATT_PAST_ROWS - n_t
        padrows = lambda a, lo, hi: jnp.pad(a, ((0, 0), (lo, hi), (0, 0))).reshape(n_bs * win_rows, ATT_WIDTH)
        q_w = padrows(q_s, ATT_PAST_ROWS, tail)
        k_w = padrows(jnp.concatenate([cache_attn_k[l].reshape(n_bs, l_c, ATT_WIDTH).astype(F32), k_s], 1), lead, tail)
        v_w = padrows(jnp.concatenate([cache_attn_v[l].reshape(n_bs, l_c, ATT_WIDTH).astype(F32), v_s], 1), lead, tail)
        yc_s = _att_call(q_w, k_w, v_w, (0, 0, 0), 0, n_bs, 3, 2, 1, lp['att_bias'])
        yc_s = yc_s.reshape(n_bs, ATT_QBLK, ATT_WIDTH)[:, :n_t].reshape(n_q, ATT_WIDTH)

        ya = jnp.concatenate([ya_p, ya_s], axis=0)
        yr = jnp.concatenate([yr_p, yr_s], axis=0)
        yc = jnp.concatenate([yc_p, yc_s], axis=0)
        merged = _merge_call(ya, yr, yc, h, lp)
        x1, x1b = _outproj_call(x, merged, lp, alpha)
        x, xb = _moe_call(x1, x1b, lp, alpha)

        keep = min(ATT_PAST_ROWS, n_s)
        kv_rows = jnp.stack([lax.slice(h, ((b + 1) * n_s - keep, OFF_ATT + ATT_WIDTH),
                                       ((b + 1) * n_s, OFF_ATT + 3 * ATT_WIDTH)) for b in range(n_bp)], 0)
        kp = kv_rows[:, :, :ATT_WIDTH]
        vp = kv_rows[:, :, ATT_WIDTH:]
        st_p = (kp.reshape(n_bp, keep, ATT_HEADS, ATT_HEAD_DIM), vp.reshape(n_bp, keep, ATT_HEADS, ATT_HEAD_DIM),
                _rwkv_state_from_pairs(rs_p), sh_p[:, 0, :RWKV_PROJ], ts_p)
        st_s = (k_s.reshape(n_bs, n_t, ATT_HEADS, ATT_HEAD_DIM), v_s.reshape(n_bs, n_t, ATT_HEADS, ATT_HEAD_DIM),
                _rwkv_state_from_pairs(rs_s), sh_s[:, 0, :RWKV_PROJ], ts_s)
        for lst, arr in zip(new_p, st_p):
            lst.append(arr)
        for lst, arr in zip(new_s, st_s):
            lst.append(arr)
    yp = x[:n_p].reshape(n_bp, n_s, D_MODEL)
    ys = x[n_p:].reshape(n_bs, n_t, D_MODEL)
    outs_p = [jnp.stack(a, 0) for a in new_p]
    outs_s = [jnp.stack(a, 0) for a in new_s]
    return (yp, ys, *outs_p, *outs_s)
```

```python
import functools
import math

import jax
import jax.numpy as jnp
import numpy as np
from jax import lax
from jax.experimental import pallas as pl
from jax.experimental.pallas import tpu as pltpu

F32 = jnp.float32
BF16 = jnp.bfloat16

D_MODEL = 2048
PAST_LEN = 1024
CHUNK = 64
RWKV_WIDTH = 1024
RWKV_HEAD_DIM = 64
RWKV_HEADS = 16
RWKV_DECAY_LORA = 64
RWKV_A_LORA = 64
RWKV_GATE_LORA = 160
RWKV_PROJ = 3 * RWKV_WIDTH + RWKV_DECAY_LORA + RWKV_A_LORA + RWKV_GATE_LORA
RWKV_PROJ_PAD = 3584
RWKV_GN_EPS = 64e-5
RET_HEAD_DIM = 128
RET_WIDTH = 1024
RET_HEADS = 8
RET_GN_EPS = 1e-5
ROPE_BASE = 10000.0
RET_CHUNK = 64
ATT_HEAD_DIM = 64
ATT_WIDTH = 1024
ATT_HEADS = 16
ATT_LEFT_CHUNKS = 8
ATT_PAST_ROWS = ATT_LEFT_CHUNKS * CHUNK
ATT_QBLK = 256
ATT_WIN = ATT_QBLK + ATT_PAST_ROWS
REL_CLIP = 128
OFF_RET = RWKV_PROJ_PAD
OFF_ATT = OFF_RET + 4 * RET_WIDTH
OFF_GATE = OFF_ATT + 3 * ATT_WIDTH
IN_PROJ_PAD = OFF_GATE + 3 * D_MODEL
N_EXPERTS = 64
TOP_K = 8
N_GROUPS = 8
TOPK_GROUPS = 4
EXPERT_DIM = 512
SHARED_DIM = 512
ROUTED_SCALE = 2.5
MOE_BLOCK = 512
LN_EPS = 1e-5
NEG_INF = -1e30
LANES = 128

VMEM_LIMIT = 56 * 1024 * 1024


def _cparams(sem):
    return pltpu.CompilerParams(dimension_semantics=sem, vmem_limit_bytes=VMEM_LIMIT)


def _tile(n, target, align=8):
    for t in range(min(n, target), 0, -1):
        if n % t == 0 and t % align == 0:
            return t
    return n


def _dot(a, b):
    return jnp.dot(a, b, preferred_element_type=F32)


def _dot_nt(a, b):
    return lax.dot_general(a, b, (((1,), (1,)), ((), ())), preferred_element_type=F32)


def _dot_tn(a, b):
    return lax.dot_general(a, b, (((0,), (0,)), ((), ())), preferred_element_type=F32)


def _sigmoid(x):
    return 1.0 / (1.0 + jnp.exp(-x))


def _split3(x):
    hi = x.astype(BF16)
    r1 = x - hi.astype(F32)
    mid = r1.astype(BF16)
    lo = (r1 - mid.astype(F32)).astype(BF16)
    return hi, mid, lo


def _mm_kernel(x_ref, w_ref, o_ref):
    o_ref[...] = _dot(x_ref[...], w_ref[...])


def _matmul(x, w, tm_target=1280, tn_target=768):
    m, k = x.shape
    n = w.shape[1]
    tm = _tile(m, tm_target)
    tn = _tile(n, tn_target, LANES)
    return pl.pallas_call(
        _mm_kernel,
        out_shape=jax.ShapeDtypeStruct((m, n), F32),
        grid=(m // tm, n // tn),
        in_specs=[pl.BlockSpec((tm, k), lambda i, j: (i, 0)),
                  pl.BlockSpec((k, tn), lambda i, j: (0, j))],
        out_specs=pl.BlockSpec((tm, tn), lambda i, j: (i, j)),
        compiler_params=_cparams(("parallel", "parallel")),
        name="in_proj",
    )(x, w)


def _rwkv_kernel(*refs, n_rows):
    p_refs = refs[:n_rows]
    (shift_ref, s0_ref, mu_ref, w0_ref, w2_ref, a0_ref, a2_ref, g2_ref, kk_ref, ka_ref, rk_ref,
     lng_ref, lnb_ref, e_ref, et_ref,
     y_ref, sout_ref, shout_ref,
     s_scr, sb_scr, prev_scr, w_scr, a_scr, r_scr, b_scr, k_scr, v_scr, yo_scr, g_scr) = refs[n_rows:]
    c = pl.program_id(1)
    n_c = pl.num_programs(1)
    C = p_refs[0].shape[0]
    n_pairs = RWKV_WIDTH // LANES
    hd = RWKV_HEAD_DIM
    ROW_W, ROW_R, ROW_B, ROW_K, ROW_V = range(5)

    @pl.when(c == 0)
    def _():
        s_scr[...] = s0_ref[...]
        sb_scr[...] = s0_ref[...].astype(BF16)
        prev_scr[...] = shift_ref[...]

    e_m = e_ref[...]
    et_m = et_ref[...]

    def headsum(x):
        s = None
        for limb in _split3(x):
            t = _dot(limb, e_m)
            s = t if s is None else s + t
        out = None
        for limb in _split3(s):
            t = _dot(limb, et_m)
            out = t if out is None else out + t
        return out

    w = RWKV_WIDTH
    for rr in range(n_rows):
        pf = p_refs[rr][...]
        row = lax.broadcasted_iota(jnp.int32, pf.shape, 0)
        prev = jnp.where(row == 0, prev_scr[rr, 0:1, :], pltpu.roll(pf, 1, 0))
        prev_scr[rr, 0:1, :] = pf[C - 1:C, :]
        px = pf + (prev - pf) * mu_ref[...]
        r = px[:, 0:w]
        k = px[:, w:2 * w]
        v = px[:, 2 * w:3 * w]
        lora = px[:, 3 * w:3 * w + LANES]
        xg = px[:, 3 * w + LANES:RWKV_PROJ_PAD]
        z = w0_ref[...] + _dot(jnp.tanh(lora).astype(BF16), w2_ref[...])
        w_log = -(jnp.maximum(-z, 0.0) + jnp.log1p(jnp.exp(-jnp.abs(z)))) - 0.5
        a = _sigmoid(a0_ref[...] + _dot(lora.astype(BF16), a2_ref[...]))
        g_scr[rr] = _dot(_sigmoid(xg).astype(BF16), g2_ref[...])
        kk = k * kk_ref[...]
        kk = kk / jnp.maximum(jnp.sqrt(headsum(kk * kk)), 1e-12)
        w_scr[rr] = jnp.exp(-jnp.exp(w_log))
        a_scr[rr] = -kk
        r_scr[rr] = r
        b_scr[rr] = kk * a
        k_scr[rr] = k * (1.0 + (a - 1.0) * ka_ref[...])
        v_scr[rr] = v
    yo_scr[...] = jnp.zeros(yo_scr.shape, F32)

    lane8 = lax.broadcasted_iota(jnp.int32, (8, LANES), 1)
    sub8 = lax.broadcasted_iota(jnp.int32, (8, LANES), 0)
    sub8h = sub8[:, 0:hd]
    hsel4 = jnp.logical_and((lane8 // hd) == (sub8 & 1), sub8 < 4)
    left_lo = (sub8 & 3) >= 2
    left_lo_h = left_lo[:, 0:hd]
    right_lo = (sub8 & 1) == 1
    right_head = (lane8 // hd) == (sub8 // 4)
    pairs = [(rr, p) for rr in range(n_rows) for p in range(n_pairs)]
    row_scrs = (w_scr, r_scr, b_scr, k_scr, v_scr)

    def limbs(x):
        hi = x.astype(BF16).astype(F32)
        return hi, x - hi

    def step(t, sas):
        t0 = pl.multiple_of((t // 8) * 8, 8)
        tj = t - t0
        t_nx = jnp.minimum(t + 1, C - 1)
        t0_nx = pl.multiple_of((t_nx // 8) * 8, 8)
        tj_nx = t_nx - t0_nx
        xs = []
        for rr, p in pairs:
            ls = slice(p * LANES, (p + 1) * LANES)
            x = None
            for q, scr in enumerate(row_scrs):
                tile = pltpu.roll(scr[rr, pl.ds(t0, 8), ls], (q + 8 - tj) & 7, 0)
                x = tile if x is None else jnp.where(sub8 == q, tile, x)
            xs.append(x)
        upds = []
        for x, sa2 in zip(xs, sas):
            x_hi, x_lo = limbs(x)
            s_hi, s_lo = limbs(sa2)
            l_sa = jnp.where(left_lo_h,
                             jnp.where(sub8h < 4, s_lo[2:3, :], s_lo[3:4, :]),
                             jnp.where(sub8h < 4, s_hi[2:3, :], s_hi[3:4, :]))
            v_sel = jnp.where(left_lo, x_lo[ROW_V:ROW_V + 1, :], x_hi[ROW_V:ROW_V + 1, :])
            l_v = jnp.where(sub8 < 4, v_sel, pltpu.roll(v_sel, hd, 1))[:, 0:hd]
            left = jnp.concatenate([l_sa, l_v], axis=0).astype(BF16)
            r_b = jnp.where(right_head, jnp.where(right_lo, x_lo[ROW_B:ROW_B + 1, :], x_hi[ROW_B:ROW_B + 1, :]), 0.0)
            r_k = jnp.where(right_head, jnp.where(right_lo, x_lo[ROW_K:ROW_K + 1, :], x_hi[ROW_K:ROW_K + 1, :]), 0.0)
            right = jnp.concatenate([r_b, r_k], axis=0).astype(BF16)
            upds.append(_dot_tn(left, right))
        sbs = []
        for (rr, p), x, upd in zip(pairs, xs, upds):
            s_new = s_scr[rr, p] * x[ROW_W:ROW_W + 1, :] + upd
            s_scr[rr, p] = s_new
            sb = s_new.astype(BF16)
            sb_scr[rr, p] = sb
            sbs.append(sb)
        sas_nx = []
        for (rr, p), x, sb in zip(pairs, xs, sbs):
            ls = slice(p * LANES, (p + 1) * LANES)
            a_nx = pltpu.roll(a_scr[rr, pl.ds(t0_nx, 8), ls], (8 - tj_nx) & 7, 0)[0:1, :]
            lhs = jnp.where(hsel4, jnp.where(sub8 < 2, x[ROW_R:ROW_R + 1, :], a_nx), 0.0).astype(BF16)
            out = _dot_nt(lhs, sb)
            y_row = jnp.concatenate([out[0:1, :], out[1:2, :]], axis=1)
            yo_scr[rr, pl.ds(t0, 8), ls] = jnp.where(sub8 == tj, y_row, yo_scr[rr, pl.ds(t0, 8), ls])
            sas_nx.append(out)
        return tuple(sas_nx)

    sa_init = []
    for rr, p in pairs:
        ls = slice(p * LANES, (p + 1) * LANES)
        lhs = jnp.where(hsel4, a_scr[rr, 0:1, ls], 0.0).astype(BF16)
        sa_init.append(_dot_nt(lhs, sb_scr[rr, p]))
    lax.fori_loop(0, C, step, tuple(sa_init))

    inv_n = 1.0 / hd
    for rr in range(n_rows):
        y = yo_scr[rr]
        mean = headsum(y) * inv_n
        d = y - mean
        var = headsum(d * d) * inv_n
        yn = d * lax.rsqrt(var + RWKV_GN_EPS) * lng_ref[...] + lnb_ref[...]
        bonus = headsum(r_scr[rr] * k_scr[rr] * rk_ref[...]) * v_scr[rr]
        y_ref[rr] = ((yn + bonus) * g_scr[rr]).astype(y_ref.dtype)

    @pl.when(c == n_c - 1)
    def _():
        sout_ref[...] = s_scr[...]
        shout_ref[...] = prev_scr[...]


def _rwkv_consts():
    lane = np.arange(RWKV_WIDTH)
    e = (lane[:, None] // RWKV_HEAD_DIM == np.arange(LANES)[None, :]).astype(np.float32)
    return jnp.asarray(e, BF16), jnp.asarray(e.T, BF16)


def _rwkv_call(h, row_blk0, n_b, n_t, shift0, s0_pairs, lp):
    C = CHUNK
    n_c = n_t // C
    n_rows = 4 if n_b % 4 == 0 else (2 if n_b % 2 == 0 else 1)
    n_pairs = RWKV_WIDTH // LANES
    consts = _rwkv_consts()
    full = lambda arr: pl.BlockSpec(arr.shape, lambda i, c: (0,) * arr.ndim)
    params = [lp['rwkv_mu'], lp['rwkv_w0'], lp['rwkv_w2'], lp['rwkv_a0'], lp['rwkv_a2'], lp['rwkv_g2'],
              lp['rwkv_k_k'], lp['rwkv_k_a'], lp['rwkv_r_k'], lp['rwkv_ln_g'], lp['rwkv_ln_b']]

    def pspec(rr):
        return pl.BlockSpec((C, RWKV_PROJ_PAD), lambda i, c: (row_blk0 + (i * n_rows + rr) * n_c + c, 0))
    state_spec = pl.BlockSpec((n_rows, n_pairs, RWKV_HEAD_DIM, LANES), lambda i, c: (i, 0, 0, 0))
    shift_spec = pl.BlockSpec((n_rows, 8, RWKV_PROJ_PAD), lambda i, c: (i, 0, 0))
    rows_scr = pltpu.VMEM((n_rows, C, RWKV_WIDTH), F32)
    y, s_out, sh_out = pl.pallas_call(
        functools.partial(_rwkv_kernel, n_rows=n_rows),
        out_shape=(jax.ShapeDtypeStruct((n_b, n_t, RWKV_WIDTH), BF16),
                   jax.ShapeDtypeStruct((n_b, n_pairs, RWKV_HEAD_DIM, LANES), F32),
                   jax.ShapeDtypeStruct((n_b, 8, RWKV_PROJ_PAD), F32)),
        grid=(n_b // n_rows, n_c),
        in_specs=[pspec(rr) for rr in range(n_rows)] + [shift_spec, state_spec]
                 + [full(a) for a in params] + [full(a) for a in consts],
        out_specs=(pl.BlockSpec((n_rows, C, RWKV_WIDTH), lambda i, c: (i, c, 0)), state_spec, shift_spec),
        scratch_shapes=[pltpu.VMEM((n_rows, n_pairs, RWKV_HEAD_DIM, LANES), F32),
                        pltpu.VMEM((n_rows, n_pairs, RWKV_HEAD_DIM, LANES), BF16),
                        pltpu.VMEM((n_rows, 8, RWKV_PROJ_PAD), F32)] + [rows_scr] * 8,
        compiler_params=_cparams(("parallel", "arbitrary")),
        name="rwkv7_scan",
    )(*([h] * n_rows), shift0, s0_pairs, *params, *consts)
    return y.reshape(n_b * n_t, RWKV_WIDTH), s_out, sh_out


def _rwkv_state_to_pairs(s):
    n_b = s.shape[0]
    s5 = s.astype(F32).reshape(n_b, RWKV_HEADS // 2, 2, RWKV_HEAD_DIM, RWKV_HEAD_DIM)
    return s5.transpose(0, 1, 3, 2, 4).reshape(n_b, RWKV_HEADS // 2, RWKV_HEAD_DIM, LANES)


def _rwkv_state_from_pairs(sp):
    n_b = sp.shape[0]
    s5 = sp.reshape(n_b, RWKV_HEADS // 2, RWKV_HEAD_DIM, 2, RWKV_HEAD_DIM)
    return s5.transpose(0, 1, 3, 2, 4).reshape(n_b, RWKV_HEADS, RWKV_HEAD_DIM, RWKV_HEAD_DIM)


def _ret_kernel(*refs, chunk_decay):
    (q0, q1, k0, k1, v0, v1, g0, g1, cc_ref, ss_ref, dec_ref, qd_ref, kd_ref, s0_ref,
     lng_ref, lnb_ref, y_ref, sout_ref, s_scr) = refs
    c = pl.program_id(1)
    n_c = pl.num_programs(1)
    n_h = s_scr.shape[0]
    hp = n_h // 2
    d = RET_HEAD_DIM

    @pl.when(c == 0)
    def _():
        s_scr[...] = s0_ref[0]

    cc = cc_ref[...]
    ss = ss_ref[...]
    outs = []
    for j in range(n_h):
        q_ref, k_ref, v_ref = ((q0, k0, v0), (q1, k1, v1))[j // hp]
        sl = slice((j % hp) * d, (j % hp + 1) * d)
        q = q_ref[:, sl]
        k = k_ref[:, sl]
        v = v_ref[:, sl].astype(BF16)
        qr = q * cc + pltpu.roll(q, d // 2, 1) * ss
        kr = (k * cc + pltpu.roll(k, d // 2, 1) * ss) * (d ** -0.5)
        qb = qr.astype(BF16)
        scores = _dot_nt(qb, kr.astype(BF16)) * dec_ref[j]
        inner = _dot(scores.astype(BF16), v)
        s_j = s_scr[j]
        cross = _dot((qr * qd_ref[j]).astype(BF16), s_j.astype(BF16))
        kv = _dot_tn((kr * kd_ref[j]).astype(BF16), v)
        s_scr[j] = s_j * chunk_decay[j] + kv
        y = inner + cross
        mu = jnp.mean(y, axis=-1, keepdims=True)
        yc = y - mu
        var = jnp.mean(yc * yc, axis=-1, keepdims=True)
        outs.append(yc * lax.rsqrt(var + RET_GN_EPS))
    yn = jnp.concatenate(outs, axis=1) * lng_ref[...] + lnb_ref[...]
    gt = jnp.concatenate([g0[...], g1[...]], axis=1)
    y_ref[...] = (gt * _sigmoid(gt) * yn).astype(y_ref.dtype)

    @pl.when(c == n_c - 1)
    def _():
        sout_ref[0] = s_scr[...]


def _ret_consts(C, pos0, n_t):
    f = np.float32
    log_gamma = np.log1p(-np.exp2(-5.0 - np.arange(RET_HEADS, dtype=f))).astype(f)
    idx = np.arange(C, dtype=f)
    diff = idx[:, None] - idx[None, :]
    dec = np.where(diff >= 0, np.exp(log_gamma[:, None, None] * np.maximum(diff, 0.0)), 0.0).astype(f)
    qd = np.exp(log_gamma[:, None] * (idx + 1.0)[None, :]).astype(f)
    kd = np.exp(log_gamma[:, None] * (C - 1.0 - idx)[None, :]).astype(f)
    qd = np.broadcast_to(qd[:, :, None], (RET_HEADS, C, RET_HEAD_DIM)).copy()
    kd = np.broadcast_to(kd[:, :, None], (RET_HEADS, C, RET_HEAD_DIM)).copy()
    cd = tuple(float(x) for x in np.exp(log_gamma * C))
    half = RET_HEAD_DIM // 2
    inv = (ROPE_BASE ** (-np.arange(half, dtype=f) / half)).astype(f)
    pos = (pos0 + np.arange(n_t)).astype(f)
    ang = pos[:, None] * inv[None, :]
    cos, sin = np.cos(ang).astype(f), np.sin(ang).astype(f)
    cc = np.concatenate([cos, cos], axis=1)
    ss = np.concatenate([-sin, sin], axis=1)
    return jnp.asarray(dec), jnp.asarray(qd), jnp.asarray(kd), cd, jnp.asarray(cc), jnp.asarray(ss)


def _ret_call(h, row_blk0, n_b, n_t, pos0, s0, lp):
    C = min(RET_CHUNK, n_t)
    n_c = n_t // C
    hw = 512
    dec, qd, kd, cd, cc, ss = _ret_consts(C, pos0, n_t)
    col0 = OFF_RET // hw
    blk = lambda col: pl.BlockSpec((C, hw), lambda b, c: (row_blk0 + b * n_c + c, col0 + col))
    full = lambda arr: pl.BlockSpec(arr.shape, lambda b, c: (0,) * arr.ndim)
    state = pl.BlockSpec((1, RET_HEADS, RET_HEAD_DIM, RET_HEAD_DIM), lambda b, c: (b, 0, 0, 0))
    y, s_out = pl.pallas_call(
        functools.partial(_ret_kernel, chunk_decay=cd),
        out_shape=(jax.ShapeDtypeStruct((n_b * n_t, RET_WIDTH), BF16),
                   jax.ShapeDtypeStruct((n_b, RET_HEADS, RET_HEAD_DIM, RET_HEAD_DIM), F32)),
        grid=(n_b, n_c),
        in_specs=[blk(col) for col in range(8)]
                 + [pl.BlockSpec((C, RET_HEAD_DIM), lambda b, c: (c, 0)),
                    pl.BlockSpec((C, RET_HEAD_DIM), lambda b, c: (c, 0)),
                    full(dec), full(qd), full(kd), state, full(lp['ret_ln_g']), full(lp['ret_ln_b'])],
        out_specs=(pl.BlockSpec((C, RET_WIDTH), lambda b, c: (b * n_c + c, 0)), state),
        scratch_shapes=[pltpu.VMEM((RET_HEADS, RET_HEAD_DIM, RET_HEAD_DIM), F32)],
        compiler_params=_cparams(("parallel", "arbitrary")),
        name="retention_chunk",
    )(*([h] * 8), cc, ss, dec, qd, kd, s0.astype(F32), lp['ret_ln_g'], lp['ret_ln_b'])
    return y, s_out


def _att_kernel(q_ref, k0_ref, k1_ref, k2_ref, v0_ref, v1_ref, v2_ref, bias_ref, o_ref, *, blk0):
    blk = pl.program_id(2) + blk0
    hd = ATT_HEAD_DIM
    n_pairs = q_ref.shape[1] // LANES
    lane = lax.broadcasted_iota(jnp.int32, (1, LANES), 1)
    k_refs = (k0_ref, k1_ref, k2_ref)
    v_refs = (v0_ref, v1_ref, v2_ref)
    valid = (blk >= 2, blk >= 1, None)
    outs = []
    for p in range(n_pairs):
        sl = slice(p * LANES, (p + 1) * LANES)
        q = q_ref[:, sl]
        ks = [kr[:, sl].astype(BF16) for kr in k_refs]
        vs = [vr[:, sl].astype(BF16) for vr in v_refs]
        o_heads = []
        for j in range(2):
            qm = jnp.where((lane // hd) == j, q, 0.0).astype(BF16)
            parts = []
            for kb in range(3):
                s = _dot_nt(qm, ks[kb]) * (hd ** -0.5)
                s = s + bias_ref[2 * p + j, :, kb * ATT_QBLK:(kb + 1) * ATT_QBLK]
                if valid[kb] is not None:
                    s = jnp.where(valid[kb], s, NEG_INF)
                parts.append(s)
            s = jnp.concatenate(parts, axis=1)
            m = jnp.max(s, axis=-1, keepdims=True)
            e = jnp.exp(s - m)
            pr = (e / jnp.sum(e, axis=-1, keepdims=True)).astype(BF16)
            o = _dot(pr[:, 0:ATT_QBLK], vs[0])
            o = o + _dot(pr[:, ATT_QBLK:2 * ATT_QBLK], vs[1])
            o = o + _dot(pr[:, 2 * ATT_QBLK:3 * ATT_QBLK], vs[2])
            o_heads.append(o)
        outs.append(jnp.where((lane // hd) == 0, o_heads[0], o_heads[1]))
    o_ref[...] = jnp.concatenate(outs, axis=1).astype(o_ref.dtype)


def _att_bias(table):
    n_h = table.shape[0]
    r = np.arange(ATT_QBLK)[:, None]
    w = np.arange(ATT_WIN)[None, :]
    lo = CHUNK * (r // CHUNK)
    band = (w >= lo) & (w < lo + ATT_PAST_ROWS + CHUNK)
    period = ATT_QBLK + ATT_WIN + 1
    j = np.arange(period)
    d_idx = np.clip(ATT_QBLK - j + ATT_PAST_ROWS, -REL_CLIP, REL_CLIP) + REL_CLIP
    d = table.astype(F32)[:, d_idx]
    skew = jnp.tile(d, (1, ATT_QBLK))[:, :ATT_QBLK * (period - 1)].reshape(n_h, ATT_QBLK, period - 1)
    b = skew[:, :, ATT_QBLK:ATT_QBLK + ATT_WIN]
    return jnp.where(jnp.asarray(band)[None], b, NEG_INF)


def _att_call(q_arr, k_arr, v_arr, cols, row_blk0, n_b, n_blk_batch, blk0, n_blk, bias):
    gw = 256
    n_hg = ATT_WIDTH // gw
    hpg = gw // ATT_HEAD_DIM
    qc, kc, vc = cols
    rowb = lambda b, i: row_blk0 + b * n_blk_batch + i
    qspec = pl.BlockSpec((ATT_QBLK, gw), lambda g, b, i: (rowb(b, i + blk0), qc + g))

    def kvspec(col, back):
        return pl.BlockSpec((ATT_QBLK, gw),
                            lambda g, b, i: (rowb(b, jnp.maximum(i + blk0 - back, 0)), col + g))
    return pl.pallas_call(
        functools.partial(_att_kernel, blk0=blk0),
        out_shape=jax.ShapeDtypeStruct((n_b * n_blk * ATT_QBLK, ATT_WIDTH), BF16),
        grid=(n_hg, n_b, n_blk),
        in_specs=[qspec, kvspec(kc, 2), kvspec(kc, 1), kvspec(kc, 0),
                  kvspec(vc, 2), kvspec(vc, 1), kvspec(vc, 0),
                  pl.BlockSpec((hpg, ATT_QBLK, ATT_WIN), lambda g, b, i: (g, 0, 0))],
        out_specs=pl.BlockSpec((ATT_QBLK, gw), lambda g, b, i: (b * n_blk + i, g)),
        compiler_params=_cparams(("parallel", "parallel", "arbitrary")),
        name="band_attention",
    )(q_arr, k_arr, k_arr, k_arr, v_arr, v_arr, v_arr, bias)


def _merge_kernel(ya_ref, yr_ref, yc_ref, ga_ref, gr_ref, gc_ref, wa_ref, wr_ref, wc_ref, o_ref):
    m = _sigmoid(ga_ref[...]) * _dot(ya_ref[...], wa_ref[...])
    m = m + _sigmoid(gr_ref[...]) * _dot(yr_ref[...], wr_ref[...])
    m = m + _sigmoid(gc_ref[...]) * _dot(yc_ref[...], wc_ref[...])
    o_ref[...] = m.astype(o_ref.dtype)


def _merge_call(ya, yr, yc, h, lp):
    m = ya.shape[0]
    tm = _tile(m, 512)
    tn = 512
    n_n = D_MODEL // tn
    g0 = OFF_GATE // tn
    yspec = pl.BlockSpec((tm, RWKV_WIDTH), lambda i, j: (i, 0))
    wspec = pl.BlockSpec((RWKV_WIDTH, tn), lambda i, j: (0, j))
    gspec = lambda part: pl.BlockSpec((tm, tn), lambda i, j: (i, g0 + part * n_n + j))
    return pl.pallas_call(
        _merge_kernel,
        out_shape=jax.ShapeDtypeStruct((m, D_MODEL), BF16),
        grid=(m // tm, n_n),
        in_specs=[yspec, yspec, yspec, gspec(0), gspec(1), gspec(2), wspec, wspec, wspec],
        out_specs=pl.BlockSpec((tm, tn), lambda i, j: (i, j)),
        compiler_params=_cparams(("parallel", "parallel")),
        name="branch_merge",
    )(ya, yr, yc, h, h, h, lp['w_branch_rwkv'], lp['w_branch_ret'], lp['w_branch_att'])


def _layer_norm(z, g, b):
    mu = jnp.mean(z, axis=-1, keepdims=True)
    zc = z - mu
    var = jnp.mean(zc * zc, axis=-1, keepdims=True)
    return zc * lax.rsqrt(var + LN_EPS) * g + b


def _outproj_kernel(x_ref, m_ref, w_ref, g_ref, b_ref, o_ref, ob_ref, *, alpha):
    z = alpha * x_ref[...] + _dot(m_ref[...], w_ref[...])
    y = _layer_norm(z, g_ref[...], b_ref[...])
    o_ref[...] = y
    ob_ref[...] = y.astype(ob_ref.dtype)


def _outproj_call(x, merged, lp, alpha):
    m = x.shape[0]
    tm = _tile(m, 256)
    row = pl.BlockSpec((tm, D_MODEL), lambda i: (i, 0))
    vec = pl.BlockSpec((1, D_MODEL), lambda i: (0, 0))
    return pl.pallas_call(
        functools.partial(_outproj_kernel, alpha=alpha),
        out_shape=(jax.ShapeDtypeStruct((m, D_MODEL), F32), jax.ShapeDtypeStruct((m, D_MODEL), BF16)),
        grid=(m // tm,),
        in_specs=[row, row, pl.BlockSpec((D_MODEL, D_MODEL), lambda i: (0, 0)), vec, vec],
        out_specs=(row, row),
        compiler_params=_cparams(("parallel",)),
        name="out_proj_ln1",
    )(x, merged, lp['w_out'], lp['ln1_g'], lp['ln1_b'])


def _router_kernel(x_ref, w_ref, b_ref, idx_ref, gate_ref):
    x = x_ref[...]
    logits = _dot(x, w_ref[...])
    scores = _sigmoid(logits)
    tm = x.shape[0]
    lane_i = lax.broadcasted_iota(jnp.int32, (tm, LANES), 1)
    lane = lane_i.astype(F32)
    real = lane_i < N_EXPERTS
    sel = jnp.where(real, scores + b_ref[...], NEG_INF)
    per_group = N_EXPERTS // N_GROUPS
    grp = (lane_i // per_group).astype(F32)

    def first_argmax(vals):
        m = jnp.max(vals, axis=-1, keepdims=True)
        i = jnp.min(jnp.where(vals == m, lane, float(LANES)), axis=-1, keepdims=True)
        return m, i

    gscore = jnp.full((tm, LANES), NEG_INF, F32)
    for gidx in range(N_GROUPS):
        in_g = grp == gidx
        vals = jnp.where(in_g, sel, -jnp.inf)
        m1, i1 = first_argmax(vals)
        m2 = jnp.max(jnp.where(lane == i1, -jnp.inf, vals), axis=-1, keepdims=True)
        gscore = jnp.where(in_g, m1 + m2, gscore)
    chosen = jnp.zeros((tm, LANES), jnp.bool_)
    cand = jnp.where(real, gscore, -jnp.inf)
    for _ in range(TOPK_GROUPS):
        _, i = first_argmax(cand)
        pick = grp == jnp.floor(i * (1.0 / per_group))
        chosen = jnp.logical_or(chosen, pick)
        cand = jnp.where(pick, -jnp.inf, cand)
    cand = jnp.where(real, jnp.where(chosen, sel, NEG_INF), -jnp.inf)
    idx_out = jnp.zeros((tm, LANES), F32)
    w_out = jnp.zeros((tm, LANES), F32)
    for kk in range(TOP_K):
        _, i = first_argmax(cand)
        hit = lane == i
        wk = jnp.sum(jnp.where(hit, scores, 0.0), axis=-1, keepdims=True)
        idx_out = jnp.where(lane == kk, i, idx_out)
        w_out = jnp.where(lane == kk, wk, w_out)
        cand = jnp.where(hit, -jnp.inf, cand)
    total = jnp.sum(w_out, axis=-1, keepdims=True)
    idx_ref[...] = idx_out.astype(jnp.int32)
    gate_ref[...] = w_out / total * ROUTED_SCALE


def _router_call(x, row0, m, lp):
    tm = _tile(math.gcd(m, row0) if row0 else m, 512)
    off = row0 // tm
    row = pl.BlockSpec((tm, LANES), lambda i: (i, 0))
    return pl.pallas_call(
        _router_kernel,
        out_shape=(jax.ShapeDtypeStruct((m, LANES), jnp.int32), jax.ShapeDtypeStruct((m, LANES), F32)),
        grid=(m // tm,),
        in_specs=[pl.BlockSpec((tm, D_MODEL), lambda i: (i + off, 0)),
                  pl.BlockSpec((D_MODEL, LANES), lambda i: (0, 0)),
                  pl.BlockSpec((1, LANES), lambda i: (0, 0))],
        out_specs=(row, row),
        compiler_params=_cparams(("parallel",)),
        name="router_topk",
    )(x, lp['router_w'], lp['router_bias'])


def _expert_kernel(be_ref, nu_ref, x_ref, wgu_ref, wdn_ref, o_ref, wgu_scr, wdn_scr):
    i = pl.program_id(0)

    @pl.when(jnp.logical_or(i == 0, be_ref[i] != be_ref[jnp.maximum(i - 1, 0)]))
    def _():
        wgu_scr[...] = wgu_ref[0, 0].astype(BF16)
        wdn_scr[...] = wdn_ref[0, 0].astype(BF16)

    @pl.when(i < nu_ref[0])
    def _():
        hgu = _dot(x_ref[...], wgu_scr[...])
        gt = hgu[:, :EXPERT_DIM]
        up = hgu[:, EXPERT_DIM:]
        act = (gt * _sigmoid(gt) * up).astype(BF16)
        o_ref[...] = _dot(act, wdn_scr[...]).astype(o_ref.dtype)

    @pl.when(i >= nu_ref[0])
    def _():
        o_ref[...] = jnp.zeros_like(o_ref)


def _expert_call(xs, block_expert, n_used, lp):
    n_slots = xs.shape[0]
    bm = MOE_BLOCK
    n_blocks = n_slots // bm
    layer = lp['layer']
    grid_spec = pltpu.PrefetchScalarGridSpec(
        num_scalar_prefetch=2,
        grid=(n_blocks,),
        in_specs=[pl.BlockSpec((bm, D_MODEL), lambda i, be, nu: (i, 0)),
                  pl.BlockSpec((1, 1, D_MODEL, 2 * EXPERT_DIM), lambda i, be, nu: (layer, be[i], 0, 0)),
                  pl.BlockSpec((1, 1, EXPERT_DIM, D_MODEL), lambda i, be, nu: (layer, be[i], 0, 0))],
        out_specs=pl.BlockSpec((bm, D_MODEL), lambda i, be, nu: (i, 0)),
        scratch_shapes=[pltpu.VMEM((D_MODEL, 2 * EXPERT_DIM), BF16), pltpu.VMEM((EXPERT_DIM, D_MODEL), BF16)],
    )
    return pl.pallas_call(
        _expert_kernel,
        out_shape=jax.ShapeDtypeStruct((n_slots, D_MODEL), BF16),
        grid_spec=grid_spec,
        compiler_params=_cparams(("arbitrary",)),
        name="routed_experts",
    )(block_expert, n_used, xs, lp['expert_w_gate_up'], lp['expert_w_down'])


def _combine_kernel(x_ref, xb_ref, yg_ref, gate_ref, wgu_ref, wdn_ref, g_ref, b_ref, o_ref, ob_ref, *, alpha):
    x = x_ref[...]
    hgu = _dot(xb_ref[...], wgu_ref[...])
    gt = hgu[:, :SHARED_DIM]
    up = hgu[:, SHARED_DIM:]
    moe = _dot((gt * _sigmoid(gt) * up).astype(BF16), wdn_ref[...])
    gate = gate_ref[...].astype(BF16).astype(F32)
    for kk in range(TOP_K):
        moe = moe + gate[:, kk:kk + 1] * yg_ref[kk].astype(F32)
    y = _layer_norm(alpha * x + moe, g_ref[...], b_ref[...])
    o_ref[...] = y
    ob_ref[...] = y.astype(ob_ref.dtype)


def _combine_call(x, xb, row0, m, yg, gate, lp, alpha):
    tm = _tile(math.gcd(m, row0) if row0 else m, 128)
    off = row0 // tm
    row = pl.BlockSpec((tm, D_MODEL), lambda i: (i, 0))
    row_in = pl.BlockSpec((tm, D_MODEL), lambda i: (i + off, 0))
    vec = pl.BlockSpec((1, D_MODEL), lambda i: (0, 0))
    return pl.pallas_call(
        functools.partial(_combine_kernel, alpha=alpha),
        out_shape=(jax.ShapeDtypeStruct((m, D_MODEL), F32), jax.ShapeDtypeStruct((m, D_MODEL), BF16)),
        grid=(m // tm,),
        in_specs=[row_in, row_in,
                  pl.BlockSpec((TOP_K, tm, D_MODEL), lambda i: (0, i, 0)),
                  pl.BlockSpec((tm, LANES), lambda i: (i, 0)),
                  pl.BlockSpec((D_MODEL, 2 * SHARED_DIM), lambda i: (0, 0)),
                  pl.BlockSpec((SHARED_DIM, D_MODEL), lambda i: (0, 0)), vec, vec],
        out_specs=(row, row),
        compiler_params=_cparams(("parallel",)),
        name="moe_combine_ln2",
    )(x, xb, yg, gate, lp['shared_w_gate_up'], lp['shared_w_down'], lp['ln2_g'], lp['ln2_b'])


def _dispatch_plan(idx):
    n_tok = idx.shape[0]
    n_a = n_tok * TOP_K
    bm = MOE_BLOCK
    n_blocks = n_a // bm + N_EXPERTS
    e_flat = idx.reshape(n_a)
    order = jnp.argsort(e_flat, stable=True).astype(jnp.int32)
    rank = jnp.argsort(order).astype(jnp.int32)
    counts = jnp.sum(e_flat[:, None] == jnp.arange(N_EXPERTS, dtype=jnp.int32)[None, :], axis=0,
                     dtype=jnp.int32)
    padded = (counts + bm - 1) // bm * bm
    pad_end = jnp.cumsum(padded)
    pad_start = pad_end - padded
    start = jnp.cumsum(counts) - counts
    slot_of_assign = pad_start[e_flat] + rank - start[e_flat]
    blk_start = jnp.arange(n_blocks, dtype=jnp.int32) * bm
    block_expert = jnp.minimum(jnp.sum(pad_end[None, :] <= blk_start[:, None], axis=1, dtype=jnp.int32),
                               N_EXPERTS - 1)
    n_used = (pad_end[-1] // bm).astype(jnp.int32).reshape(1)
    last_e = block_expert[jnp.maximum(n_used[0] - 1, 0)]
    block_expert = jnp.where(jnp.arange(n_blocks) < n_used[0], block_expert, last_e)
    slot_e = jnp.repeat(block_expert, bm)
    j = jnp.arange(n_blocks * bm, dtype=jnp.int32) - pad_start[slot_e]
    src = order[jnp.clip(start[slot_e] + j, 0, n_a - 1)] // TOP_K
    slot_tok = jnp.where(j < counts[slot_e], src, 0)
    return slot_tok, slot_of_assign, block_expert, n_used


def _moe_call(x1, x1b, row0, n_tok, lp, alpha):
    idx_p, gate_p = _router_call(x1b, row0, n_tok, lp)
    idx = idx_p[:, :TOP_K]
    slot_tok, slot_of_assign, block_expert, n_used = _dispatch_plan(idx)
    xs = x1b.at[slot_tok + row0].get(mode='promise_in_bounds')
    ys = _expert_call(xs, block_expert, n_used, lp)
    slot_kmajor = slot_of_assign.reshape(n_tok, TOP_K).T.reshape(-1)
    yg = ys.at[slot_kmajor].get(mode='promise_in_bounds').reshape(TOP_K, n_tok, D_MODEL)
    return _combine_call(x1, x1b, row0, n_tok, yg, gate_p, lp, alpha)


def _prep_layer(l, w_in, rwkv_mu, rwkv_w0, rwkv_w2, rwkv_a0, rwkv_a2, rwkv_g2, rwkv_k_k, rwkv_k_a,
                rwkv_r_k, rwkv_ln_g, rwkv_ln_b, ret_ln_g, ret_ln_b, att_rel_bias, w_branch_rwkv,
                w_branch_ret, w_branch_att, w_out, ln1_g, ln1_b, router_w, router_bias,
                expert_w_gate_up, expert_w_down, shared_w_gate_up, shared_w_down, ln2_g, ln2_b):
    pad_c = RWKV_PROJ_PAD - RWKV_PROJ
    wi = w_in[l]
    wi = jnp.concatenate([wi[:, :RWKV_PROJ], jnp.zeros((D_MODEL, pad_c), wi.dtype), wi[:, RWKV_PROJ:]], axis=1)
    rowv = lambda a: a.reshape(1, -1).astype(F32)
    zrows = lambda n: jnp.zeros((n, RWKV_WIDTH), F32)
    g_rows = RWKV_PROJ_PAD - 3 * RWKV_WIDTH - LANES
    return {
        'w_in': wi.astype(BF16),
        'rwkv_mu': jnp.pad(rowv(rwkv_mu[l]), ((0, 0), (0, pad_c))),
        'rwkv_w0': rowv(rwkv_w0[l]),
        'rwkv_w2': jnp.concatenate([rwkv_w2[l], zrows(RWKV_A_LORA)], 0).astype(BF16),
        'rwkv_a0': rowv(rwkv_a0[l]),
        'rwkv_a2': jnp.concatenate([zrows(RWKV_DECAY_LORA), rwkv_a2[l]], 0).astype(BF16),
        'rwkv_g2': jnp.concatenate([rwkv_g2[l], zrows(g_rows - RWKV_GATE_LORA)], 0).astype(BF16),
        'rwkv_k_k': rowv(rwkv_k_k[l]), 'rwkv_k_a': rowv(rwkv_k_a[l]), 'rwkv_r_k': rowv(rwkv_r_k[l]),
        'rwkv_ln_g': rowv(rwkv_ln_g[l]), 'rwkv_ln_b': rowv(rwkv_ln_b[l]),
        'ret_ln_g': rowv(ret_ln_g[l]), 'ret_ln_b': rowv(ret_ln_b[l]),
        'att_bias': _att_bias(att_rel_bias[l]),
        'w_branch_rwkv': w_branch_rwkv[l].astype(BF16), 'w_branch_ret': w_branch_ret[l].astype(BF16),
        'w_branch_att': w_branch_att[l].astype(BF16), 'w_out': w_out[l].astype(BF16),
        'ln1_g': rowv(ln1_g[l]), 'ln1_b': rowv(ln1_b[l]),
        'router_w': jnp.pad(router_w[l], ((0, 0), (0, LANES - N_EXPERTS))).astype(BF16),
        'router_bias': jnp.pad(rowv(router_bias[l]), ((0, 0), (0, LANES - N_EXPERTS))),
        'layer': l, 'expert_w_gate_up': expert_w_gate_up, 'expert_w_down': expert_w_down,
        'shared_w_gate_up': shared_w_gate_up[l].astype(BF16), 'shared_w_down': shared_w_down[l].astype(BF16),
        'ln2_g': rowv(ln2_g[l]), 'ln2_b': rowv(ln2_b[l]),
    }


def _pad_shift(shift):
    s = jnp.pad(shift.astype(F32), ((0, 0), (0, RWKV_PROJ_PAD - RWKV_PROJ)))
    return jnp.pad(s[:, None, :], ((0, 0), (0, 7), (0, 0)))


def kernel(x_prompt, x_sample, cache_attn_k, cache_attn_v, state_rwkv, state_rwkv_shift, state_ret, w_in, rwkv_mu, rwkv_w0, rwkv_w2, rwkv_a0, rwkv_a2, rwkv_g2, rwkv_k_k, rwkv_k_a, rwkv_r_k, rwkv_ln_g, rwkv_ln_b, ret_ln_g, ret_ln_b, att_rel_bias, w_branch_rwkv, w_branch_ret, w_branch_att, w_out, ln1_g, ln1_b, router_w, router_bias, expert_w_gate_up, expert_w_down, shared_w_gate_up, shared_w_down, ln2_g, ln2_b):
    n_bp, n_s, _ = x_prompt.shape
    n_bs, n_t, _ = x_sample.shape
    depth = w_in.shape[0]
    n_p = n_bp * n_s
    n_q = n_bs * n_t
    assert n_s % ATT_QBLK == 0 and n_t == CHUNK and n_p % ATT_QBLK == 0
    alpha = float((2 * depth) ** 0.25)
    weights = (w_in, rwkv_mu, rwkv_w0, rwkv_w2, rwkv_a0, rwkv_a2, rwkv_g2, rwkv_k_k, rwkv_k_a, rwkv_r_k,
               rwkv_ln_g, rwkv_ln_b, ret_ln_g, ret_ln_b, att_rel_bias, w_branch_rwkv, w_branch_ret,
               w_branch_att, w_out, ln1_g, ln1_b, router_w, router_bias, expert_w_gate_up, expert_w_down,
               shared_w_gate_up, shared_w_down, ln2_g, ln2_b)

    x = jnp.concatenate([x_prompt.reshape(n_p, D_MODEL), x_sample.reshape(n_q, D_MODEL)], axis=0)
    xb = x.astype(BF16)
    l_c = cache_attn_k.shape[2]
    assert l_c == ATT_PAST_ROWS
    new_p = [[], [], [], [], []]
    new_s = [[], [], [], [], []]
    zeros_shift = jnp.zeros((n_bp, 8, RWKV_PROJ_PAD), F32)
    zeros_rwkv = jnp.zeros((n_bp, RWKV_HEADS // 2, RWKV_HEAD_DIM, LANES), F32)
    zeros_ret = jnp.zeros((n_bp, RET_HEADS, RET_HEAD_DIM, RET_HEAD_DIM), F32)
    for l in range(depth):
        lp = _prep_layer(l, *weights)
        h = _matmul(xb, lp['w_in'])
        hq = h[n_p:]

        ya_p, rs_p, sh_p = _rwkv_call(h, 0, n_bp, n_s, zeros_shift, zeros_rwkv, lp)
        ya_s, rs_s, sh_s = _rwkv_call(h, n_p // CHUNK, n_bs, n_t, _pad_shift(state_rwkv_shift[l]),
                                      _rwkv_state_to_pairs(state_rwkv[l]), lp)
        yr_p, ts_p = _ret_call(h, 0, n_bp, n_s, 0, zeros_ret, lp)
        yr_s, ts_s = _ret_call(h, n_p // n_t, n_bs, n_t, PAST_LEN, state_ret[l], lp)
        cq = OFF_ATT // 256
        cols = (cq, cq + ATT_WIDTH // 256, cq + 2 * ATT_WIDTH // 256)
        yc_p = _att_call(h, h, h, cols, 0, n_bp, n_s // ATT_QBLK, 0, n_s // ATT_QBLK, lp['att_bias'])
        q_s = hq[:, OFF_ATT:OFF_ATT + ATT_WIDTH].reshape(n_bs, n_t, ATT_WIDTH)
        k_s = hq[:, OFF_ATT + ATT_WIDTH:OFF_ATT + 2 * ATT_WIDTH].reshape(n_bs, n_t, ATT_WIDTH)
        v_s = hq[:, OFF_ATT + 2 * ATT_WIDTH:OFF_ATT + 3 * ATT_WIDTH].reshape(n_bs, n_t, ATT_WIDTH)
        win_rows = 3 * ATT_QBLK
        lead = ATT_PAST_ROWS - l_c
        tail = win_rows - ATT_PAST_ROWS - n_t
        padrows = lambda a, lo, hi: jnp.pad(a, ((0, 0), (lo, hi), (0, 0))).reshape(n_bs * win_rows, ATT_WIDTH)
        q_w = padrows(q_s, ATT_PAST_ROWS, tail)
        k_w = padrows(jnp.concatenate([cache_attn_k[l].reshape(n_bs, l_c, ATT_WIDTH).astype(F32), k_s], 1), lead, tail)
        v_w = padrows(jnp.concatenate([cache_attn_v[l].reshape(n_bs, l_c, ATT_WIDTH).astype(F32), v_s], 1), lead, tail)
        yc_s = _att_call(q_w, k_w, v_w, (0, 0, 0), 0, n_bs, 3, 2, 1, lp['att_bias'])
        yc_s = yc_s.reshape(n_bs, ATT_QBLK, ATT_WIDTH)[:, :n_t].reshape(n_q, ATT_WIDTH)

        ya = jnp.concatenate([ya_p, ya_s], axis=0)
        yr = jnp.concatenate([yr_p, yr_s], axis=0)
        yc = jnp.concatenate([yc_p, yc_s], axis=0)
        merged = _merge_call(ya, yr, yc, h, lp)
        x1, x1b = _outproj_call(x, merged, lp, alpha)
        n_all = n_p + n_q
        cut = (n_all // 2) // ATT_QBLK * ATT_QBLK
        parts = [_moe_call(x1, x1b, r0, m, lp, alpha) for r0, m in ((0, cut), (cut, n_all - cut))]
        x = jnp.concatenate([o[0] for o in parts], axis=0)
        xb = jnp.concatenate([o[1] for o in parts], axis=0)

        keep = min(ATT_PAST_ROWS, n_s)
        kv_rows = jnp.stack([lax.slice(h, ((b + 1) * n_s - keep, OFF_ATT + ATT_WIDTH),
                                       ((b + 1) * n_s, OFF_ATT + 3 * ATT_WIDTH)) for b in range(n_bp)], 0)
        kp = kv_rows[:, :, :ATT_WIDTH]
        vp = kv_rows[:, :, ATT_WIDTH:]
        st_p = (kp.reshape(n_bp, keep, ATT_HEADS, ATT_HEAD_DIM), vp.reshape(n_bp, keep, ATT_HEADS, ATT_HEAD_DIM),
                _rwkv_state_from_pairs(rs_p), sh_p[:, 0, :RWKV_PROJ], ts_p)
        st_s = (k_s.reshape(n_bs, n_t, ATT_HEADS, ATT_HEAD_DIM), v_s.reshape(n_bs, n_t, ATT_HEADS, ATT_HEAD_DIM),
                _rwkv_state_from_pairs(rs_s), sh_s[:, 0, :RWKV_PROJ], ts_s)
        for lst, arr in zip(new_p, st_p):
            lst.append(arr)
        for lst, arr in zip(new_s, st_s):
            lst.append(arr)
    yp = x[:n_p].reshape(n_bp, n_s, D_MODEL)
    ys = x[n_p:].reshape(n_bs, n_t, D_MODEL)
    outs_p = [jnp.stack(a, 0) for a in new_p]
    outs_s = [jnp.stack(a, 0) for a in new_s]
    return (yp, ys, *outs_p, *outs_s)
```

```python
import functools
import math

import jax
import jax.numpy as jnp
import numpy as np
from jax import lax
from jax.experimental import pallas as pl
from jax.experimental.pallas import tpu as pltpu

F32 = jnp.float32
BF16 = jnp.bfloat16

D_MODEL = 2048
PAST_LEN = 1024
CHUNK = 64
RWKV_WIDTH = 1024
RWKV_HEAD_DIM = 64
RWKV_HEADS = 16
RWKV_DECAY_LORA = 64
RWKV_A_LORA = 64
RWKV_GATE_LORA = 160
RWKV_PROJ = 3 * RWKV_WIDTH + RWKV_DECAY_LORA + RWKV_A_LORA + RWKV_GATE_LORA
RWKV_PROJ_PAD = 3584
RWKV_GN_EPS = 64e-5
RET_HEAD_DIM = 128
RET_WIDTH = 1024
RET_HEADS = 8
RET_GN_EPS = 1e-5
ROPE_BASE = 10000.0
RET_CHUNK = 64
ATT_HEAD_DIM = 64
ATT_WIDTH = 1024
ATT_HEADS = 16
ATT_LEFT_CHUNKS = 8
ATT_PAST_ROWS = ATT_LEFT_CHUNKS * CHUNK
ATT_QBLK = 256
ATT_WIN = ATT_QBLK + ATT_PAST_ROWS
REL_CLIP = 128
OFF_RET = RWKV_PROJ_PAD
OFF_ATT = OFF_RET + 4 * RET_WIDTH
OFF_GATE = OFF_ATT + 3 * ATT_WIDTH
IN_PROJ_PAD = OFF_GATE + 3 * D_MODEL
N_EXPERTS = 64
TOP_K = 8
N_GROUPS = 8
TOPK_GROUPS = 4
EXPERT_DIM = 512
SHARED_DIM = 512
ROUTED_SCALE = 2.5
MOE_BLOCK = 512
LN_EPS = 1e-5
NEG_INF = -1e30
LANES = 128

VMEM_LIMIT = 56 * 1024 * 1024


def _cparams(sem):
    return pltpu.CompilerParams(dimension_semantics=sem, vmem_limit_bytes=VMEM_LIMIT)


def _tile(n, target, align=8):
    for t in range(min(n, target), 0, -1):
        if n % t == 0 and t % align == 0:
            return t
    return n


def _dot(a, b):
    return jnp.dot(a, b, preferred_element_type=F32)


def _dot_nt(a, b):
    return lax.dot_general(a, b, (((1,), (1,)), ((), ())), preferred_element_type=F32)


def _dot_tn(a, b):
    return lax.dot_general(a, b, (((0,), (0,)), ((), ())), preferred_element_type=F32)


def _sigmoid(x):
    return 1.0 / (1.0 + jnp.exp(-x))


def _split3(x):
    hi = x.astype(BF16)
    r1 = x - hi.astype(F32)
    mid = r1.astype(BF16)
    lo = (r1 - mid.astype(F32)).astype(BF16)
    return hi, mid, lo


def _mm_kernel(x_ref, w_ref, o_ref):
    o_ref[...] = _dot(x_ref[...], w_ref[...])


def _matmul(x, w, tm_target=1280, tn_target=768):
    m, k = x.shape
    n = w.shape[1]
    tm = _tile(m, tm_target)
    tn = _tile(n, tn_target, LANES)
    return pl.pallas_call(
        _mm_kernel,
        out_shape=jax.ShapeDtypeStruct((m, n), F32),
        grid=(m // tm, n // tn),
        in_specs=[pl.BlockSpec((tm, k), lambda i, j: (i, 0)),
                  pl.BlockSpec((k, tn), lambda i, j: (0, j))],
        out_specs=pl.BlockSpec((tm, tn), lambda i, j: (i, j)),
        compiler_params=_cparams(("parallel", "parallel")),
        name="in_proj",
    )(x, w)


def _rwkv_kernel(*refs, n_rows):
    p_refs = refs[:n_rows]
    (shift_ref, s0_ref, mu_ref, w0_ref, w2_ref, a0_ref, a2_ref, g2_ref, kk_ref, ka_ref, rk_ref,
     lng_ref, lnb_ref, e_ref, et_ref,
     y_ref, sout_ref, shout_ref,
     s_scr, sb_scr, prev_scr, w_scr, a_scr, r_scr, b_scr, k_scr, v_scr, yo_scr, g_scr) = refs[n_rows:]
    c = pl.program_id(1)
    n_c = pl.num_programs(1)
    C = p_refs[0].shape[0]
    n_pairs = RWKV_WIDTH // LANES
    hd = RWKV_HEAD_DIM
    ROW_W, ROW_R, ROW_B, ROW_K, ROW_V = range(5)

    @pl.when(c == 0)
    def _():
        s_scr[...] = s0_ref[...]
        sb_scr[...] = s0_ref[...].astype(BF16)
        prev_scr[...] = shift_ref[...]

    e_m = e_ref[...]
    et_m = et_ref[...]

    def headsum(x):
        s = None
        for limb in _split3(x):
            t = _dot(limb, e_m)
            s = t if s is None else s + t
        out = None
        for limb in _split3(s):
            t = _dot(limb, et_m)
            out = t if out is None else out + t
        return out

    w = RWKV_WIDTH
    for rr in range(n_rows):
        pf = p_refs[rr][...]
        row = lax.broadcasted_iota(jnp.int32, pf.shape, 0)
        prev = jnp.where(row == 0, prev_scr[rr, 0:1, :], pltpu.roll(pf, 1, 0))
        prev_scr[rr, 0:1, :] = pf[C - 1:C, :]
        px = pf + (prev - pf) * mu_ref[...]
        r = px[:, 0:w]
        k = px[:, w:2 * w]
        v = px[:, 2 * w:3 * w]
        lora = px[:, 3 * w:3 * w + LANES]
        xg = px[:, 3 * w + LANES:RWKV_PROJ_PAD]
        z = w0_ref[...] + _dot(jnp.tanh(lora).astype(BF16), w2_ref[...])
        a = _sigmoid(a0_ref[...] + _dot(lora.astype(BF16), a2_ref[...]))
        g_scr[rr] = _dot(_sigmoid(xg).astype(BF16), g2_ref[...])
        kk = k * kk_ref[...]
        kk = kk / jnp.maximum(jnp.sqrt(headsum(kk * kk)), 1e-12)
        w_scr[rr] = jnp.exp(-math.exp(-0.5) * _sigmoid(z))
        a_scr[rr] = -kk
        r_scr[rr] = r
        b_scr[rr] = kk * a
        k_scr[rr] = k * (1.0 + (a - 1.0) * ka_ref[...])
        v_scr[rr] = v
    yo_scr[...] = jnp.zeros(yo_scr.shape, F32)

    lane8 = lax.broadcasted_iota(jnp.int32, (8, LANES), 1)
    sub8 = lax.broadcasted_iota(jnp.int32, (8, LANES), 0)
    sub8h = sub8[:, 0:hd]
    hsel4 = jnp.logical_and((lane8 // hd) == (sub8 & 1), sub8 < 4)
    left_lo = (sub8 & 3) >= 2
    left_lo_h = left_lo[:, 0:hd]
    right_lo = (sub8 & 1) == 1
    right_head = (lane8 // hd) == (sub8 // 4)
    pairs = [(rr, p) for rr in range(n_rows) for p in range(n_pairs)]
    row_scrs = (w_scr, r_scr, b_scr, k_scr, v_scr)

    def limbs(x):
        hi = x.astype(BF16).astype(F32)
        return hi, x - hi

    def step(t, sas):
        t0 = pl.multiple_of((t // 8) * 8, 8)
        tj = t - t0
        t_nx = jnp.minimum(t + 1, C - 1)
        t0_nx = pl.multiple_of((t_nx // 8) * 8, 8)
        tj_nx = t_nx - t0_nx
        xs = []
        for rr, p in pairs:
            ls = slice(p * LANES, (p + 1) * LANES)
            x = None
            for q, scr in enumerate(row_scrs):
                tile = pltpu.roll(scr[rr, pl.ds(t0, 8), ls], (q + 8 - tj) & 7, 0)
                x = tile if x is None else jnp.where(sub8 == q, tile, x)
            xs.append(x)
        upds = []
        for x, sa2 in zip(xs, sas):
            x_hi, x_lo = limbs(x)
            s_hi, s_lo = limbs(sa2)
            l_sa = jnp.where(left_lo_h,
                             jnp.where(sub8h < 4, s_lo[2:3, :], s_lo[3:4, :]),
                             jnp.where(sub8h < 4, s_hi[2:3, :], s_hi[3:4, :]))
            v_sel = jnp.where(left_lo, x_lo[ROW_V:ROW_V + 1, :], x_hi[ROW_V:ROW_V + 1, :])
            l_v = jnp.where(sub8 < 4, v_sel, pltpu.roll(v_sel, hd, 1))[:, 0:hd]
            left = jnp.concatenate([l_sa, l_v], axis=0).astype(BF16)
            r_b = jnp.where(right_head, jnp.where(right_lo, x_lo[ROW_B:ROW_B + 1, :], x_hi[ROW_B:ROW_B + 1, :]), 0.0)
            r_k = jnp.where(right_head, jnp.where(right_lo, x_lo[ROW_K:ROW_K + 1, :], x_hi[ROW_K:ROW_K + 1, :]), 0.0)
            right = jnp.concatenate([r_b, r_k], axis=0).astype(BF16)
            upds.append(_dot_tn(left, right))
        sbs = []
        for (rr, p), x, upd in zip(pairs, xs, upds):
            s_new = s_scr[rr, p] * x[ROW_W:ROW_W + 1, :] + upd
            s_scr[rr, p] = s_new
            sb = s_new.astype(BF16)
            sb_scr[rr, p] = sb
            sbs.append(sb)
        sas_nx = []
        for (rr, p), x, sb in zip(pairs, xs, sbs):
            ls = slice(p * LANES, (p + 1) * LANES)
            a_nx = pltpu.roll(a_scr[rr, pl.ds(t0_nx, 8), ls], (8 - tj_nx) & 7, 0)[0:1, :]
            lhs = jnp.where(hsel4, jnp.where(sub8 < 2, x[ROW_R:ROW_R + 1, :], a_nx), 0.0).astype(BF16)
            out = _dot_nt(lhs, sb)
            y_row = jnp.concatenate([out[0:1, :], out[1:2, :]], axis=1)
            yo_scr[rr, pl.ds(t0, 8), ls] = jnp.where(sub8 == tj, y_row, yo_scr[rr, pl.ds(t0, 8), ls])
            sas_nx.append(out)
        return tuple(sas_nx)

    sa_init = []
    for rr, p in pairs:
        ls = slice(p * LANES, (p + 1) * LANES)
        lhs = jnp.where(hsel4, a_scr[rr, 0:1, ls], 0.0).astype(BF16)
        sa_init.append(_dot_nt(lhs, sb_scr[rr, p]))
    lax.fori_loop(0, C, step, tuple(sa_init))

    inv_n = 1.0 / hd
    for rr in range(n_rows):
        y = yo_scr[rr]
        mean = headsum(y) * inv_n
        d = y - mean
        var = headsum(d * d) * inv_n
        yn = d * lax.rsqrt(var + RWKV_GN_EPS) * lng_ref[...] + lnb_ref[...]
        bonus = headsum(r_scr[rr] * k_scr[rr] * rk_ref[...]) * v_scr[rr]
        y_ref[rr] = ((yn + bonus) * g_scr[rr]).astype(y_ref.dtype)

    @pl.when(c == n_c - 1)
    def _():
        sout_ref[...] = s_scr[...]
        shout_ref[...] = prev_scr[...]


def _rwkv_consts():
    lane = np.arange(RWKV_WIDTH)
    e = (lane[:, None] // RWKV_HEAD_DIM == np.arange(LANES)[None, :]).astype(np.float32)
    return jnp.asarray(e, BF16), jnp.asarray(e.T, BF16)


def _rwkv_call(h, row_blk0, n_b, n_t, shift0, s0_pairs, lp):
    C = CHUNK
    n_c = n_t // C
    n_rows = 4 if n_b % 4 == 0 else (2 if n_b % 2 == 0 else 1)
    n_pairs = RWKV_WIDTH // LANES
    consts = _rwkv_consts()
    full = lambda arr: pl.BlockSpec(arr.shape, lambda i, c: (0,) * arr.ndim)
    params = [lp['rwkv_mu'], lp['rwkv_w0'], lp['rwkv_w2'], lp['rwkv_a0'], lp['rwkv_a2'], lp['rwkv_g2'],
              lp['rwkv_k_k'], lp['rwkv_k_a'], lp['rwkv_r_k'], lp['rwkv_ln_g'], lp['rwkv_ln_b']]

    def pspec(rr):
        return pl.BlockSpec((C, RWKV_PROJ_PAD), lambda i, c: (row_blk0 + (i * n_rows + rr) * n_c + c, 0))
    state_spec = pl.BlockSpec((n_rows, n_pairs, RWKV_HEAD_DIM, LANES), lambda i, c: (i, 0, 0, 0))
    shift_spec = pl.BlockSpec((n_rows, 8, RWKV_PROJ_PAD), lambda i, c: (i, 0, 0))
    rows_scr = pltpu.VMEM((n_rows, C, RWKV_WIDTH), F32)
    y, s_out, sh_out = pl.pallas_call(
        functools.partial(_rwkv_kernel, n_rows=n_rows),
        out_shape=(jax.ShapeDtypeStruct((n_b, n_t, RWKV_WIDTH), BF16),
                   jax.ShapeDtypeStruct((n_b, n_pairs, RWKV_HEAD_DIM, LANES), F32),
                   jax.ShapeDtypeStruct((n_b, 8, RWKV_PROJ_PAD), F32)),
        grid=(n_b // n_rows, n_c),
        in_specs=[pspec(rr) for rr in range(n_rows)] + [shift_spec, state_spec]
                 + [full(a) for a in params] + [full(a) for a in consts],
        out_specs=(pl.BlockSpec((n_rows, C, RWKV_WIDTH), lambda i, c: (i, c, 0)), state_spec, shift_spec),
        scratch_shapes=[pltpu.VMEM((n_rows, n_pairs, RWKV_HEAD_DIM, LANES), F32),
                        pltpu.VMEM((n_rows, n_pairs, RWKV_HEAD_DIM, LANES), BF16),
                        pltpu.VMEM((n_rows, 8, RWKV_PROJ_PAD), F32)] + [rows_scr] * 8,
        compiler_params=_cparams(("parallel", "arbitrary")),
        name="rwkv7_scan",
    )(*([h] * n_rows), shift0, s0_pairs, *params, *consts)
    return y.reshape(n_b * n_t, RWKV_WIDTH), s_out, sh_out


def _rwkv_state_to_pairs(s):
    n_b = s.shape[0]
    s5 = s.astype(F32).reshape(n_b, RWKV_HEADS // 2, 2, RWKV_HEAD_DIM, RWKV_HEAD_DIM)
    return s5.transpose(0, 1, 3, 2, 4).reshape(n_b, RWKV_HEADS // 2, RWKV_HEAD_DIM, LANES)


def _rwkv_state_from_pairs(sp):
    n_b = sp.shape[0]
    s5 = sp.reshape(n_b, RWKV_HEADS // 2, RWKV_HEAD_DIM, 2, RWKV_HEAD_DIM)
    return s5.transpose(0, 1, 3, 2, 4).reshape(n_b, RWKV_HEADS, RWKV_HEAD_DIM, RWKV_HEAD_DIM)


def _ret_kernel(*refs, chunk_decay):
    (q0, q1, k0, k1, v0, v1, g0, g1, cc_ref, ss_ref, dec_ref, qd_ref, kd_ref, s0_ref,
     lng_ref, lnb_ref, y_ref, sout_ref, s_scr) = refs
    c = pl.program_id(1)
    n_c = pl.num_programs(1)
    n_h = s_scr.shape[0]
    hp = n_h // 2
    d = RET_HEAD_DIM

    @pl.when(c == 0)
    def _():
        s_scr[...] = s0_ref[0]

    cc = cc_ref[...]
    ss = ss_ref[...]
    outs = []
    for j in range(n_h):
        q_ref, k_ref, v_ref = ((q0, k0, v0), (q1, k1, v1))[j // hp]
        sl = slice((j % hp) * d, (j % hp + 1) * d)
        q = q_ref[:, sl]
        k = k_ref[:, sl]
        v = v_ref[:, sl].astype(BF16)
        qr = q * cc + pltpu.roll(q, d // 2, 1) * ss
        kr = (k * cc + pltpu.roll(k, d // 2, 1) * ss) * (d ** -0.5)
        qb = qr.astype(BF16)
        scores = _dot_nt(qb, kr.astype(BF16)) * dec_ref[j]
        inner = _dot(scores.astype(BF16), v)
        s_j = s_scr[j]
        cross = _dot((qr * qd_ref[j]).astype(BF16), s_j.astype(BF16))
        kv = _dot_tn((kr * kd_ref[j]).astype(BF16), v)
        s_scr[j] = s_j * chunk_decay[j] + kv
        y = inner + cross
        mu = jnp.mean(y, axis=-1, keepdims=True)
        yc = y - mu
        var = jnp.mean(yc * yc, axis=-1, keepdims=True)
        outs.append(yc * lax.rsqrt(var + RET_GN_EPS))
    yn = jnp.concatenate(outs, axis=1) * lng_ref[...] + lnb_ref[...]
    gt = jnp.concatenate([g0[...], g1[...]], axis=1)
    y_ref[...] = (gt * _sigmoid(gt) * yn).astype(y_ref.dtype)

    @pl.when(c == n_c - 1)
    def _():
        sout_ref[0] = s_scr[...]


def _ret_consts(C, pos0, n_t):
    f = np.float32
    log_gamma = np.log1p(-np.exp2(-5.0 - np.arange(RET_HEADS, dtype=f))).astype(f)
    idx = np.arange(C, dtype=f)
    diff = idx[:, None] - idx[None, :]
    dec = np.where(diff >= 0, np.exp(log_gamma[:, None, None] * np.maximum(diff, 0.0)), 0.0).astype(f)
    qd = np.exp(log_gamma[:, None] * (idx + 1.0)[None, :]).astype(f)
    kd = np.exp(log_gamma[:, None] * (C - 1.0 - idx)[None, :]).astype(f)
    qd = np.broadcast_to(qd[:, :, None], (RET_HEADS, C, RET_HEAD_DIM)).copy()
    kd = np.broadcast_to(kd[:, :, None], (RET_HEADS, C, RET_HEAD_DIM)).copy()
    cd = tuple(float(x) for x in np.exp(log_gamma * C))
    half = RET_HEAD_DIM // 2
    inv = (ROPE_BASE ** (-np.arange(half, dtype=f) / half)).astype(f)
    pos = (pos0 + np.arange(n_t)).astype(f)
    ang = pos[:, None] * inv[None, :]
    cos, sin = np.cos(ang).astype(f), np.sin(ang).astype(f)
    cc = np.concatenate([cos, cos], axis=1)
    ss = np.concatenate([-sin, sin], axis=1)
    return jnp.asarray(dec), jnp.asarray(qd), jnp.asarray(kd), cd, jnp.asarray(cc), jnp.asarray(ss)


def _ret_call(h, row_blk0, n_b, n_t, pos0, s0, lp):
    C = min(RET_CHUNK, n_t)
    n_c = n_t // C
    hw = 512
    dec, qd, kd, cd, cc, ss = _ret_consts(C, pos0, n_t)
    col0 = OFF_RET // hw
    blk = lambda col: pl.BlockSpec((C, hw), lambda b, c: (row_blk0 + b * n_c + c, col0 + col))
    full = lambda arr: pl.BlockSpec(arr.shape, lambda b, c: (0,) * arr.ndim)
    state = pl.BlockSpec((1, RET_HEADS, RET_HEAD_DIM, RET_HEAD_DIM), lambda b, c: (b, 0, 0, 0))
    y, s_out = pl.pallas_call(
        functools.partial(_ret_kernel, chunk_decay=cd),
        out_shape=(jax.ShapeDtypeStruct((n_b * n_t, RET_WIDTH), BF16),
                   jax.ShapeDtypeStruct((n_b, RET_HEADS, RET_HEAD_DIM, RET_HEAD_DIM), F32)),
        grid=(n_b, n_c),
        in_specs=[blk(col) for col in range(8)]
                 + [pl.BlockSpec((C, RET_HEAD_DIM), lambda b, c: (c, 0)),
                    pl.BlockSpec((C, RET_HEAD_DIM), lambda b, c: (c, 0)),
                    full(dec), full(qd), full(kd), state, full(lp['ret_ln_g']), full(lp['ret_ln_b'])],
        out_specs=(pl.BlockSpec((C, RET_WIDTH), lambda b, c: (b * n_c + c, 0)), state),
        scratch_shapes=[pltpu.VMEM((RET_HEADS, RET_HEAD_DIM, RET_HEAD_DIM), F32)],
        compiler_params=_cparams(("parallel", "arbitrary")),
        name="retention_chunk",
    )(*([h] * 8), cc, ss, dec, qd, kd, s0.astype(F32), lp['ret_ln_g'], lp['ret_ln_b'])
    return y, s_out


def _att_kernel(q_ref, k0_ref, k1_ref, k2_ref, v0_ref, v1_ref, v2_ref, bias_ref, o_ref, *, blk0):
    blk = pl.program_id(2) + blk0
    hd = ATT_HEAD_DIM
    n_pairs = q_ref.shape[1] // LANES
    lane = lax.broadcasted_iota(jnp.int32, (1, LANES), 1)
    k_refs = (k0_ref, k1_ref, k2_ref)
    v_refs = (v0_ref, v1_ref, v2_ref)
    valid = (blk >= 2, blk >= 1, None)
    outs = []
    for p in range(n_pairs):
        sl = slice(p * LANES, (p + 1) * LANES)
        q = q_ref[:, sl]
        ks = [kr[:, sl].astype(BF16) for kr in k_refs]
        vs = [vr[:, sl].astype(BF16) for vr in v_refs]
        o_heads = []
        for j in range(2):
            qm = jnp.where((lane // hd) == j, q, 0.0).astype(BF16)
            parts = []
            for kb in range(3):
                s = _dot_nt(qm, ks[kb]) * (hd ** -0.5)
                s = s + bias_ref[2 * p + j, :, kb * ATT_QBLK:(kb + 1) * ATT_QBLK]
                if valid[kb] is not None:
                    s = jnp.where(valid[kb], s, NEG_INF)
                parts.append(s)
            s = jnp.concatenate(parts, axis=1)
            m = jnp.max(s, axis=-1, keepdims=True)
            e = jnp.exp(s - m)
            pr = (e / jnp.sum(e, axis=-1, keepdims=True)).astype(BF16)
            o = _dot(pr[:, 0:ATT_QBLK], vs[0])
            o = o + _dot(pr[:, ATT_QBLK:2 * ATT_QBLK], vs[1])
            o = o + _dot(pr[:, 2 * ATT_QBLK:3 * ATT_QBLK], vs[2])
            o_heads.append(o)
        outs.append(jnp.where((lane // hd) == 0, o_heads[0], o_heads[1]))
    o_ref[...] = jnp.concatenate(outs, axis=1).astype(o_ref.dtype)


def _att_bias(table):
    n_h = table.shape[0]
    r = np.arange(ATT_QBLK)[:, None]
    w = np.arange(ATT_WIN)[None, :]
    lo = CHUNK * (r // CHUNK)
    band = (w >= lo) & (w < lo + ATT_PAST_ROWS + CHUNK)
    period = ATT_QBLK + ATT_WIN + 1
    j = np.arange(period)
    d_idx = np.clip(ATT_QBLK - j + ATT_PAST_ROWS, -REL_CLIP, REL_CLIP) + REL_CLIP
    d = table.astype(F32)[:, d_idx]
    skew = jnp.tile(d, (1, ATT_QBLK))[:, :ATT_QBLK * (period - 1)].reshape(n_h, ATT_QBLK, period - 1)
    b = skew[:, :, ATT_QBLK:ATT_QBLK + ATT_WIN]
    return jnp.where(jnp.asarray(band)[None], b, NEG_INF)


def _att_call(q_arr, k_arr, v_arr, cols, row_blk0, n_b, n_blk_batch, blk0, n_blk, bias):
    gw = 256
    n_hg = ATT_WIDTH // gw
    hpg = gw // ATT_HEAD_DIM
    qc, kc, vc = cols
    rowb = lambda b, i: row_blk0 + b * n_blk_batch + i
    qspec = pl.BlockSpec((ATT_QBLK, gw), lambda g, b, i: (rowb(b, i + blk0), qc + g))

    def kvspec(col, back):
        return pl.BlockSpec((ATT_QBLK, gw),
                            lambda g, b, i: (rowb(b, jnp.maximum(i + blk0 - back, 0)), col + g))
    return pl.pallas_call(
        functools.partial(_att_kernel, blk0=blk0),
        out_shape=jax.ShapeDtypeStruct((n_b * n_blk * ATT_QBLK, ATT_WIDTH), BF16),
        grid=(n_hg, n_b, n_blk),
        in_specs=[qspec, kvspec(kc, 2), kvspec(kc, 1), kvspec(kc, 0),
                  kvspec(vc, 2), kvspec(vc, 1), kvspec(vc, 0),
                  pl.BlockSpec((hpg, ATT_QBLK, ATT_WIN), lambda g, b, i: (g, 0, 0))],
        out_specs=pl.BlockSpec((ATT_QBLK, gw), lambda g, b, i: (b * n_blk + i, g)),
        compiler_params=_cparams(("parallel", "parallel", "arbitrary")),
        name="band_attention",
    )(q_arr, k_arr, k_arr, k_arr, v_arr, v_arr, v_arr, bias)


def _merge_kernel(ya_ref, yr_ref, yc_ref, ga_ref, gr_ref, gc_ref, wa_ref, wr_ref, wc_ref, o_ref):
    m = _sigmoid(ga_ref[...]) * _dot(ya_ref[...], wa_ref[...])
    m = m + _sigmoid(gr_ref[...]) * _dot(yr_ref[...], wr_ref[...])
    m = m + _sigmoid(gc_ref[...]) * _dot(yc_ref[...], wc_ref[...])
    o_ref[...] = m.astype(o_ref.dtype)


def _merge_call(ya, yr, yc, h, lp):
    m = ya.shape[0]
    tm = _tile(m, 512)
    tn = 512
    n_n = D_MODEL // tn
    g0 = OFF_GATE // tn
    yspec = pl.BlockSpec((tm, RWKV_WIDTH), lambda i, j: (i, 0))
    wspec = pl.BlockSpec((RWKV_WIDTH, tn), lambda i, j: (0, j))
    gspec = lambda part: pl.BlockSpec((tm, tn), lambda i, j: (i, g0 + part * n_n + j))
    return pl.pallas_call(
        _merge_kernel,
        out_shape=jax.ShapeDtypeStruct((m, D_MODEL), BF16),
        grid=(m // tm, n_n),
        in_specs=[yspec, yspec, yspec, gspec(0), gspec(1), gspec(2), wspec, wspec, wspec],
        out_specs=pl.BlockSpec((tm, tn), lambda i, j: (i, j)),
        compiler_params=_cparams(("parallel", "parallel")),
        name="branch_merge",
    )(ya, yr, yc, h, h, h, lp['w_branch_rwkv'], lp['w_branch_ret'], lp['w_branch_att'])


def _layer_norm(z, g, b):
    mu = jnp.mean(z, axis=-1, keepdims=True)
    zc = z - mu
    var = jnp.mean(zc * zc, axis=-1, keepdims=True)
    return zc * lax.rsqrt(var + LN_EPS) * g + b


def _outproj_kernel(x_ref, m_ref, w_ref, g_ref, b_ref, o_ref, ob_ref, *, alpha):
    z = alpha * x_ref[...] + _dot(m_ref[...], w_ref[...])
    y = _layer_norm(z, g_ref[...], b_ref[...])
    o_ref[...] = y
    ob_ref[...] = y.astype(ob_ref.dtype)


def _outproj_call(x, merged, lp, alpha):
    m = x.shape[0]
    tm = _tile(m, 256)
    row = pl.BlockSpec((tm, D_MODEL), lambda i: (i, 0))
    vec = pl.BlockSpec((1, D_MODEL), lambda i: (0, 0))
    return pl.pallas_call(
        functools.partial(_outproj_kernel, alpha=alpha),
        out_shape=(jax.ShapeDtypeStruct((m, D_MODEL), F32), jax.ShapeDtypeStruct((m, D_MODEL), BF16)),
        grid=(m // tm,),
        in_specs=[row, row, pl.BlockSpec((D_MODEL, D_MODEL), lambda i: (0, 0)), vec, vec],
        out_specs=(row, row),
        compiler_params=_cparams(("parallel",)),
        name="out_proj_ln1",
    )(x, merged, lp['w_out'], lp['ln1_g'], lp['ln1_b'])


def _router_kernel(x_ref, w_ref, b_ref, idx_ref, gate_ref):
    x = x_ref[...]
    logits = _dot(x, w_ref[...])
    scores = _sigmoid(logits)
    tm = x.shape[0]
    lane_i = lax.broadcasted_iota(jnp.int32, (tm, LANES), 1)
    lane = lane_i.astype(F32)
    real = lane_i < N_EXPERTS
    sel = jnp.where(real, scores + b_ref[...], NEG_INF)
    per_group = N_EXPERTS // N_GROUPS
    grp = (lane_i // per_group).astype(F32)

    def first_argmax(vals):
        m = jnp.max(vals, axis=-1, keepdims=True)
        i = jnp.min(jnp.where(vals == m, lane, float(LANES)), axis=-1, keepdims=True)
        return m, i

    gscore = jnp.full((tm, LANES), NEG_INF, F32)
    for gidx in range(N_GROUPS):
        in_g = grp == gidx
        vals = jnp.where(in_g, sel, -jnp.inf)
        m1, i1 = first_argmax(vals)
        m2 = jnp.max(jnp.where(lane == i1, -jnp.inf, vals), axis=-1, keepdims=True)
        gscore = jnp.where(in_g, m1 + m2, gscore)
    chosen = jnp.zeros((tm, LANES), jnp.bool_)
    cand = jnp.where(real, gscore, -jnp.inf)
    for _ in range(TOPK_GROUPS):
        _, i = first_argmax(cand)
        pick = grp == jnp.floor(i * (1.0 / per_group))
        chosen = jnp.logical_or(chosen, pick)
        cand = jnp.where(pick, -jnp.inf, cand)
    cand = jnp.where(real, jnp.where(chosen, sel, NEG_INF), -jnp.inf)
    idx_out = jnp.zeros((tm, LANES), F32)
    w_out = jnp.zeros((tm, LANES), F32)
    for kk in range(TOP_K):
        _, i = first_argmax(cand)
        hit = lane == i
        wk = jnp.sum(jnp.where(hit, scores, 0.0), axis=-1, keepdims=True)
        idx_out = jnp.where(lane == kk, i, idx_out)
        w_out = jnp.where(lane == kk, wk, w_out)
        cand = jnp.where(hit, -jnp.inf, cand)
    total = jnp.sum(w_out, axis=-1, keepdims=True)
    idx_ref[...] = idx_out.astype(jnp.int32)
    gate_ref[...] = w_out / total * ROUTED_SCALE


def _router_call(x, lp):
    m = x.shape[0]
    tm = _tile(m, 512)
    row = pl.BlockSpec((tm, LANES), lambda i: (i, 0))
    return pl.pallas_call(
        _router_kernel,
        out_shape=(jax.ShapeDtypeStruct((m, LANES), jnp.int32), jax.ShapeDtypeStruct((m, LANES), F32)),
        grid=(m // tm,),
        in_specs=[pl.BlockSpec((tm, D_MODEL), lambda i: (i, 0)),
                  pl.BlockSpec((D_MODEL, LANES), lambda i: (0, 0)),
                  pl.BlockSpec((1, LANES), lambda i: (0, 0))],
        out_specs=(row, row),
        compiler_params=_cparams(("parallel",)),
        name="router_topk",
    )(x, lp['router_w'], lp['router_bias'])


def _expert_kernel(be_ref, nu_ref, x_ref, wgu_ref, wdn_ref, o_ref, wgu_scr, wdn_scr):
    i = pl.program_id(0)

    @pl.when(jnp.logical_or(i == 0, be_ref[i] != be_ref[jnp.maximum(i - 1, 0)]))
    def _():
        wgu_scr[...] = wgu_ref[0, 0].astype(BF16)
        wdn_scr[...] = wdn_ref[0, 0].astype(BF16)

    @pl.when(i < nu_ref[0])
    def _():
        hgu = _dot(x_ref[...], wgu_scr[...])
        gt = hgu[:, :EXPERT_DIM]
        up = hgu[:, EXPERT_DIM:]
        act = (gt * _sigmoid(gt) * up).astype(BF16)
        o_ref[...] = _dot(act, wdn_scr[...]).astype(o_ref.dtype)

    @pl.when(i >= nu_ref[0])
    def _():
        o_ref[...] = jnp.zeros_like(o_ref)


def _expert_call(xs, block_expert, n_used, lp):
    n_slots = xs.shape[0]
    bm = MOE_BLOCK
    n_blocks = n_slots // bm
    layer = lp['layer']
    grid_spec = pltpu.PrefetchScalarGridSpec(
        num_scalar_prefetch=2,
        grid=(n_blocks,),
        in_specs=[pl.BlockSpec((bm, D_MODEL), lambda i, be, nu: (i, 0)),
                  pl.BlockSpec((1, 1, D_MODEL, 2 * EXPERT_DIM), lambda i, be, nu: (layer, be[i], 0, 0)),
                  pl.BlockSpec((1, 1, EXPERT_DIM, D_MODEL), lambda i, be, nu: (layer, be[i], 0, 0))],
        out_specs=pl.BlockSpec((bm, D_MODEL), lambda i, be, nu: (i, 0)),
        scratch_shapes=[pltpu.VMEM((D_MODEL, 2 * EXPERT_DIM), BF16), pltpu.VMEM((EXPERT_DIM, D_MODEL), BF16)],
    )
    return pl.pallas_call(
        _expert_kernel,
        out_shape=jax.ShapeDtypeStruct((n_slots, D_MODEL), BF16),
        grid_spec=grid_spec,
        compiler_params=_cparams(("arbitrary",)),
        name="routed_experts",
    )(block_expert, n_used, xs, lp['expert_w_gate_up'], lp['expert_w_down'])


def _combine_kernel(x_ref, xb_ref, yg_ref, gate_ref, wgu_ref, wdn_ref, g_ref, b_ref, o_ref, ob_ref, *, alpha):
    x = x_ref[...]
    hgu = _dot(xb_ref[...], wgu_ref[...])
    gt = hgu[:, :SHARED_DIM]
    up = hgu[:, SHARED_DIM:]
    moe = _dot((gt * _sigmoid(gt) * up).astype(BF16), wdn_ref[...])
    gate = gate_ref[...].astype(BF16).astype(F32)
    for kk in range(TOP_K):
        moe = moe + gate[:, kk:kk + 1] * yg_ref[kk].astype(F32)
    y = _layer_norm(alpha * x + moe, g_ref[...], b_ref[...])
    o_ref[...] = y
    ob_ref[...] = y.astype(ob_ref.dtype)


def _combine_call(x, xb, yg, gate, lp, alpha):
    m = x.shape[0]
    tm = _tile(m, 128)
    row = pl.BlockSpec((tm, D_MODEL), lambda i: (i, 0))
    vec = pl.BlockSpec((1, D_MODEL), lambda i: (0, 0))
    return pl.pallas_call(
        functools.partial(_combine_kernel, alpha=alpha),
        out_shape=(jax.ShapeDtypeStruct((m, D_MODEL), F32), jax.ShapeDtypeStruct((m, D_MODEL), BF16)),
        grid=(m // tm,),
        in_specs=[row, row,
                  pl.BlockSpec((TOP_K, tm, D_MODEL), lambda i: (0, i, 0)),
                  pl.BlockSpec((tm, LANES), lambda i: (i, 0)),
                  pl.BlockSpec((D_MODEL, 2 * SHARED_DIM), lambda i: (0, 0)),
                  pl.BlockSpec((SHARED_DIM, D_MODEL), lambda i: (0, 0)), vec, vec],
        out_specs=(row, row),
        compiler_params=_cparams(("parallel",)),
        name="moe_combine_ln2",
    )(x, xb, yg, gate, lp['shared_w_gate_up'], lp['shared_w_down'], lp['ln2_g'], lp['ln2_b'])


def _dispatch_plan(idx):
    n_tok = idx.shape[0]
    n_a = n_tok * TOP_K
    bm = MOE_BLOCK
    n_blocks = n_a // bm + N_EXPERTS
    e_flat = idx.reshape(n_a)
    order = jnp.argsort(e_flat, stable=True).astype(jnp.int32)
    rank = jnp.argsort(order).astype(jnp.int32)
    counts = jnp.sum(e_flat[:, None] == jnp.arange(N_EXPERTS, dtype=jnp.int32)[None, :], axis=0,
                     dtype=jnp.int32)
    padded = (counts + bm - 1) // bm * bm
    pad_end = jnp.cumsum(padded)
    pad_start = pad_end - padded
    start = jnp.cumsum(counts) - counts
    slot_of_assign = pad_start[e_flat] + rank - start[e_flat]
    blk_start = jnp.arange(n_blocks, dtype=jnp.int32) * bm
    block_expert = jnp.minimum(jnp.sum(pad_end[None, :] <= blk_start[:, None], axis=1, dtype=jnp.int32),
                               N_EXPERTS - 1)
    n_used = (pad_end[-1] // bm).astype(jnp.int32).reshape(1)
    last_e = block_expert[jnp.maximum(n_used[0] - 1, 0)]
    block_expert = jnp.where(jnp.arange(n_blocks) < n_used[0], block_expert, last_e)
    slot_e = jnp.repeat(block_expert, bm)
    j = jnp.arange(n_blocks * bm, dtype=jnp.int32) - pad_start[slot_e]
    src = order[jnp.clip(start[slot_e] + j, 0, n_a - 1)] // TOP_K
    slot_tok = jnp.where(j < counts[slot_e], src, 0)
    return slot_tok, slot_of_assign, block_expert, n_used


def _moe_call(x1, x1b, lp, alpha):
    idx_p, gate_p = _router_call(x1b, lp)
    idx = idx_p[:, :TOP_K]
    slot_tok, slot_of_assign, block_expert, n_used = _dispatch_plan(idx)
    n_tok = x1.shape[0]
    xs = x1b.at[slot_tok].get(mode='promise_in_bounds')
    ys = _expert_call(xs, block_expert, n_used, lp)
    slot_kmajor = slot_of_assign.reshape(n_tok, TOP_K).T.reshape(-1)
    yg = ys.at[slot_kmajor].get(mode='promise_in_bounds').reshape(TOP_K, n_tok, D_MODEL)
    return _combine_call(x1, x1b, yg, gate_p, lp, alpha)


def _prep_layer(l, w_in, rwkv_mu, rwkv_w0, rwkv_w2, rwkv_a0, rwkv_a2, rwkv_g2, rwkv_k_k, rwkv_k_a,
                rwkv_r_k, rwkv_ln_g, rwkv_ln_b, ret_ln_g, ret_ln_b, att_rel_bias, w_branch_rwkv,
                w_branch_ret, w_branch_att, w_out, ln1_g, ln1_b, router_w, router_bias,
                expert_w_gate_up, expert_w_down, shared_w_gate_up, shared_w_down, ln2_g, ln2_b):
    pad_c = RWKV_PROJ_PAD - RWKV_PROJ
    wi = w_in[l]
    wi = jnp.concatenate([wi[:, :RWKV_PROJ], jnp.zeros((D_MODEL, pad_c), wi.dtype), wi[:, RWKV_PROJ:]], axis=1)
    rowv = lambda a: a.reshape(1, -1).astype(F32)
    zrows = lambda n: jnp.zeros((n, RWKV_WIDTH), F32)
    g_rows = RWKV_PROJ_PAD - 3 * RWKV_WIDTH - LANES
    return {
        'w_in': wi.astype(BF16),
        'rwkv_mu': jnp.pad(rowv(rwkv_mu[l]), ((0, 0), (0, pad_c))),
        'rwkv_w0': rowv(rwkv_w0[l]),
        'rwkv_w2': jnp.concatenate([rwkv_w2[l], zrows(RWKV_A_LORA)], 0).astype(BF16),
        'rwkv_a0': rowv(rwkv_a0[l]),
        'rwkv_a2': jnp.concatenate([zrows(RWKV_DECAY_LORA), rwkv_a2[l]], 0).astype(BF16),
        'rwkv_g2': jnp.concatenate([rwkv_g2[l], zrows(g_rows - RWKV_GATE_LORA)], 0).astype(BF16),
        'rwkv_k_k': rowv(rwkv_k_k[l]), 'rwkv_k_a': rowv(rwkv_k_a[l]), 'rwkv_r_k': rowv(rwkv_r_k[l]),
        'rwkv_ln_g': rowv(rwkv_ln_g[l]), 'rwkv_ln_b': rowv(rwkv_ln_b[l]),
        'ret_ln_g': rowv(ret_ln_g[l]), 'ret_ln_b': rowv(ret_ln_b[l]),
        'att_bias': _att_bias(att_rel_bias[l]),
        'w_branch_rwkv': w_branch_rwkv[l].astype(BF16), 'w_branch_ret': w_branch_ret[l].astype(BF16),
        'w_branch_att': w_branch_att[l].astype(BF16), 'w_out': w_out[l].astype(BF16),
        'ln1_g': rowv(ln1_g[l]), 'ln1_b': rowv(ln1_b[l]),
        'router_w': jnp.pad(router_w[l], ((0, 0), (0, LANES - N_EXPERTS))).astype(BF16),
        'router_bias': jnp.pad(rowv(router_bias[l]), ((0, 0), (0, LANES - N_EXPERTS))),
        'layer': l, 'expert_w_gate_up': expert_w_gate_up, 'expert_w_down': expert_w_down,
        'shared_w_gate_up': shared_w_gate_up[l].astype(BF16), 'shared_w_down': shared_w_down[l].astype(BF16),
        'ln2_g': rowv(ln2_g[l]), 'ln2_b': rowv(ln2_b[l]),
    }


def _pad_shift(shift):
    s = jnp.pad(shift.astype(F32), ((0, 0), (0, RWKV_PROJ_PAD - RWKV_PROJ)))
    return jnp.pad(s[:, None, :], ((0, 0), (0, 7), (0, 0)))


def kernel(x_prompt, x_sample, cache_attn_k, cache_attn_v, state_rwkv, state_rwkv_shift, state_ret, w_in, rwkv_mu, rwkv_w0, rwkv_w2, rwkv_a0, rwkv_a2, rwkv_g2, rwkv_k_k, rwkv_k_a, rwkv_r_k, rwkv_ln_g, rwkv_ln_b, ret_ln_g, ret_ln_b, att_rel_bias, w_branch_rwkv, w_branch_ret, w_branch_att, w_out, ln1_g, ln1_b, router_w, router_bias, expert_w_gate_up, expert_w_down, shared_w_gate_up, shared_w_down, ln2_g, ln2_b):
    n_bp, n_s, _ = x_prompt.shape
    n_bs, n_t, _ = x_sample.shape
    depth = w_in.shape[0]
    n_p = n_bp * n_s
    n_q = n_bs * n_t
    assert n_s % ATT_QBLK == 0 and n_t == CHUNK and n_p % ATT_QBLK == 0
    alpha = float((2 * depth) ** 0.25)
    weights = (w_in, rwkv_mu, rwkv_w0, rwkv_w2, rwkv_a0, rwkv_a2, rwkv_g2, rwkv_k_k, rwkv_k_a, rwkv_r_k,
               rwkv_ln_g, rwkv_ln_b, ret_ln_g, ret_ln_b, att_rel_bias, w_branch_rwkv, w_branch_ret,
               w_branch_att, w_out, ln1_g, ln1_b, router_w, router_bias, expert_w_gate_up, expert_w_down,
               shared_w_gate_up, shared_w_down, ln2_g, ln2_b)

    x = jnp.concatenate([x_prompt.reshape(n_p, D_MODEL), x_sample.reshape(n_q, D_MODEL)], axis=0)
    xb = x.astype(BF16)
    l_c = cache_attn_k.shape[2]
    assert l_c == ATT_PAST_ROWS
    new_p = [[], [], [], [], []]
    new_s = [[], [], [], [], []]
    zeros_shift = jnp.zeros((n_bp, 8, RWKV_PROJ_PAD), F32)
    zeros_rwkv = jnp.zeros((n_bp, RWKV_HEADS // 2, RWKV_HEAD_DIM, LANES), F32)
    zeros_ret = jnp.zeros((n_bp, RET_HEADS, RET_HEAD_DIM, RET_HEAD_DIM), F32)
    for l in range(depth):
        lp = _prep_layer(l, *weights)
        h = _matmul(xb, lp['w_in'])
        hq = h[n_p:]

        ya_p, rs_p, sh_p = _rwkv_call(h, 0, n_bp, n_s, zeros_shift, zeros_rwkv, lp)
        ya_s, rs_s, sh_s = _rwkv_call(h, n_p // CHUNK, n_bs, n_t, _pad_shift(state_rwkv_shift[l]),
                                      _rwkv_state_to_pairs(state_rwkv[l]), lp)
        yr_p, ts_p = _ret_call(h, 0, n_bp, n_s, 0, zeros_ret, lp)
        yr_s, ts_s = _ret_call(h, n_p // n_t, n_bs, n_t, PAST_LEN, state_ret[l], lp)
        cq = OFF_ATT // 256
        cols = (cq, cq + ATT_WIDTH // 256, cq + 2 * ATT_WIDTH // 256)
        yc_p = _att_call(h, h, h, cols, 0, n_bp, n_s // ATT_QBLK, 0, n_s // ATT_QBLK, lp['att_bias'])
        q_s = hq[:, OFF_ATT:OFF_ATT + ATT_WIDTH].reshape(n_bs, n_t, ATT_WIDTH)
        k_s = hq[:, OFF_ATT + ATT_WIDTH:OFF_ATT + 2 * ATT_WIDTH].reshape(n_bs, n_t, ATT_WIDTH)
        v_s = hq[:, OFF_ATT + 2 * ATT_WIDTH:OFF_ATT + 3 * ATT_WIDTH].reshape(n_bs, n_t, ATT_WIDTH)
        win_rows = 3 * ATT_QBLK
        lead = ATT_PAST_ROWS - l_c
        tail = win_rows - ATT_PAST_ROWS - n_t
        padrows = lambda a, lo, hi: jnp.pad(a, ((0, 0), (lo, hi), (0, 0))).reshape(n_bs * win_rows, ATT_WIDTH)
        q_w = padrows(q_s, ATT_PAST_ROWS, tail)
        k_w = padrows(jnp.concatenate([cache_attn_k[l].reshape(n_bs, l_c, ATT_WIDTH).astype(F32), k_s], 1), lead, tail)
        v_w = padrows(jnp.concatenate([cache_attn_v[l].reshape(n_bs, l_c, ATT_WIDTH).astype(F32), v_s], 1), lead, tail)
        yc_s = _att_call(q_w, k_w, v_w, (0, 0, 0), 0, n_bs, 3, 2, 1, lp['att_bias'])
        yc_s = yc_s.reshape(n_bs, ATT_QBLK, ATT_WIDTH)[:, :n_t].reshape(n_q, ATT_WIDTH)

        ya = jnp.concatenate([ya_p, ya_s], axis=0)
        yr = jnp.concatenate([yr_p, yr_s], axis=0)
        yc = jnp.concatenate([yc_p, yc_s], axis=0)
        merged = _merge_call(ya, yr, yc, h, lp)
        x1, x1b = _outproj_call(x, merged, lp, alpha)
        x, xb = _moe_call(x1, x1b, lp, alpha)

        keep = min(ATT_PAST_ROWS, n_s)
        kv_rows = jnp.stack([lax.slice(h, ((b + 1) * n_s - keep, OFF_ATT + ATT_WIDTH),
                                       ((b + 1) * n_s, OFF_ATT + 3 * ATT_WIDTH)) for b in range(n_bp)], 0)
        kp = kv_rows[:, :, :ATT_WIDTH]
        vp = kv_rows[:, :, ATT_WIDTH:]
        st_p = (kp.reshape(n_bp, keep, ATT_HEADS, ATT_HEAD_DIM), vp.reshape(n_bp, keep, ATT_HEADS, ATT_HEAD_DIM),
                _rwkv_state_from_pairs(rs_p), sh_p[:, 0, :RWKV_PROJ], ts_p)
        st_s = (k_s.reshape(n_bs, n_t, ATT_HEADS, ATT_HEAD_DIM), v_s.reshape(n_bs, n_t, ATT_HEADS, ATT_HEAD_DIM),
                _rwkv_state_from_pairs(rs_s), sh_s[:, 0, :RWKV_PROJ], ts_s)
        for lst, arr in zip(new_p, st_p):
            lst.append(arr)
        for lst, arr in zip(new_s, st_s):
            lst.append(arr)
    yp = x[:n_p].reshape(n_bp, n_s, D_MODEL)
    ys = x[n_p:].reshape(n_bs, n_t, D_MODEL)
    outs_p = [jnp.stack(a, 0) for a in new_p]
    outs_s = [jnp.stack(a, 0) for a in new_s]
    return (yp, ys, *outs_p, *outs_s)
```
